```python
import math
import jax, jax.numpy as jnp
from jax import lax
import numpy as np

D_MODEL = 2048
BATCH = 4
SEQ = 2048
DEPTH = 1

CHUNK = 64
N_META = 16
D_SSM = D_MODEL // 2
SSM_GROUP = 16
N_SSM_GROUPS = D_SSM // SSM_GROUP
SSM_STATE = 64
D_CONV = D_MODEL // 2
CONV_WIDTH = 31
D_FF = 4 * D_MODEL
EPS = 1e-6
DT_MIN = 1e-3
DT_MAX = 1e-1

kernel_name = "s5_conformer_gated_hybrid"


def rms_norm(x, g):
    xf = x.astype(jnp.float32)
    y = xf * lax.rsqrt(jnp.mean(xf * xf, axis=-1, keepdims=True) + EPS)
    return (y * g.astype(jnp.float32)).astype(x.dtype)


def layer_norm(x, g, b):
    xf = x.astype(jnp.float32)
    mu = jnp.mean(xf, axis=-1, keepdims=True)
    xc = xf - mu
    y = xc * lax.rsqrt(jnp.mean(xc * xc, axis=-1, keepdims=True) + EPS)
    return (y * g.astype(jnp.float32) + b.astype(jnp.float32)).astype(x.dtype)


def s5_mixer(u, lam_re, lam_im, log_dt, b_re, b_im, c_re, c_im, d_skip):
    bsz, seq_len, _ = u.shape
    uf = u.astype(jnp.float32)
    ug = uf.reshape(bsz, seq_len, N_SSM_GROUPS, SSM_GROUP)
    lam = lax.complex(lam_re.astype(jnp.float32), lam_im.astype(jnp.float32))
    dt = jnp.exp(log_dt.astype(jnp.float32))[:, None]
    lam_bar = jnp.exp(lam * dt)
    b_cplx = lax.complex(b_re.astype(jnp.float32), b_im.astype(jnp.float32))
    b_bar = ((lam_bar - 1.0) / lam)[..., None] * b_cplx
    bu = jnp.einsum('gnh,blgh->blgn', b_bar, ug.astype(jnp.complex64))
    a = jnp.broadcast_to(lam_bar, bu.shape)

    def combine(e1, e2):
        a1, s1 = e1
        a2, s2 = e2
        return a1 * a2, a2 * s1 + s2

    _, states = lax.associative_scan(combine, (a, bu), axis=1)
    y = (jnp.einsum('ghn,blgn->blgh', c_re.astype(jnp.float32), states.real)
         - jnp.einsum('ghn,blgn->blgh', c_im.astype(jnp.float32), states.imag))
    return y.reshape(bsz, seq_len, D_SSM) + d_skip.astype(jnp.float32) * uf


def causal_depthwise_conv(x, w, b):
    k = w.shape[0]
    xp = jnp.pad(x, ((0, 0), (k - 1, 0), (0, 0)))
    y = lax.conv_general_dilated(xp, w[:, None, :].astype(x.dtype), window_strides=(1,),
                                 padding='VALID', dimension_numbers=('NWC', 'WIO', 'NWC'),
                                 feature_group_count=x.shape[-1])
    return y + b.astype(x.dtype)


def setup_inputs(seed: int = 0) -> dict:
    key = jax.random.key(seed)
    ks = jax.random.split(key, 24)
    f32 = jnp.float32
    n_in = D_SSM + 2 * D_CONV + 2 * D_MODEL
    nrm = lambda k, shape, scale: (jax.random.normal(k, shape, f32) * scale).astype(f32)
    x = jax.random.normal(ks[0], (BATCH, SEQ, D_MODEL), f32)
    meta = nrm(ks[1], (N_META, D_MODEL), 1.0)
    norm_mix_g = 1.0 + nrm(ks[2], (DEPTH, D_MODEL), 0.02)
    w_in = nrm(ks[3], (DEPTH, D_MODEL, n_in), D_MODEL ** -0.5)
    lam_re = -0.5 + nrm(ks[4], (DEPTH, N_SSM_GROUPS, SSM_STATE), 0.01)
    lam_im = (math.pi * jnp.arange(SSM_STATE, dtype=f32))[None, None, :] + nrm(ks[5], (DEPTH, N_SSM_GROUPS, SSM_STATE), 0.01)
    log_dt = jax.random.uniform(ks[6], (DEPTH, N_SSM_GROUPS), f32, math.log(DT_MIN), math.log(DT_MAX))
    b_re = nrm(ks[7], (DEPTH, N_SSM_GROUPS, SSM_STATE, SSM_GROUP), (2 * SSM_GROUP) ** -0.5)
    b_im = nrm(ks[8], (DEPTH, N_SSM_GROUPS, SSM_STATE, SSM_GROUP), (2 * SSM_GROUP) ** -0.5)
    c_re = nrm(ks[9], (DEPTH, N_SSM_GROUPS, SSM_GROUP, SSM_STATE), (2 * SSM_STATE) ** -0.5)
    c_im = nrm(ks[10], (DEPTH, N_SSM_GROUPS, SSM_GROUP, SSM_STATE), (2 * SSM_STATE) ** -0.5)
    d_skip = nrm(ks[11], (DEPTH, D_SSM), 1.0)
    w_glu = nrm(ks[12], (DEPTH, D_SSM, 2 * D_MODEL), D_SSM ** -0.5)
    conv_w = nrm(ks[13], (DEPTH, CONV_WIDTH, D_CONV), CONV_WIDTH ** -0.5)
    conv_b = nrm(ks[14], (DEPTH, D_CONV), 0.01)
    conv_ln_g = 1.0 + nrm(ks[15], (DEPTH, D_CONV), 0.02)
    conv_ln_b = nrm(ks[16], (DEPTH, D_CONV), 0.01)
    w_conv_out = nrm(ks[17], (DEPTH, D_CONV, D_MODEL), D_CONV ** -0.5)
    w_out = nrm(ks[18], (DEPTH, D_MODEL, D_MODEL), D_MODEL ** -0.5)
    norm_ffn_g = 1.0 + nrm(ks[19], (DEPTH, D_MODEL), 0.02)
    w_ff1 = nrm(ks[20], (DEPTH, D_MODEL, D_FF), D_MODEL ** -0.5)
    w_ff2 = nrm(ks[21], (DEPTH, D_FF, D_MODEL), D_FF ** -0.5)
    norm_f_g = 1.0 + nrm(ks[22], (D_MODEL,), 0.02)
    return {"x": x, "meta": meta, "norm_mix_g": norm_mix_g, "w_in": w_in,
            "lam_re": lam_re, "lam_im": lam_im, "log_dt": log_dt,
            "b_re": b_re, "b_im": b_im, "c_re": c_re, "c_im": c_im, "d_skip": d_skip,
            "w_glu": w_glu, "conv_w": conv_w, "conv_b": conv_b,
            "conv_ln_g": conv_ln_g, "conv_ln_b": conv_ln_b, "w_conv_out": w_conv_out,
            "w_out": w_out, "norm_ffn_g": norm_ffn_g, "w_ff1": w_ff1, "w_ff2": w_ff2,
            "norm_f_g": norm_f_g}


def reference(x, meta, norm_mix_g, w_in, lam_re, lam_im, log_dt, b_re, b_im, c_re, c_im, d_skip,
              w_glu, conv_w, conv_b, conv_ln_g, conv_ln_b, w_conv_out, w_out, norm_ffn_g,
              w_ff1, w_ff2, norm_f_g):
    bsz = x.shape[0]
    h_res = jnp.concatenate([jnp.broadcast_to(meta.astype(x.dtype)[None], (bsz, N_META, D_MODEL)), x], axis=1)
    split_pts = [D_SSM, D_SSM + 2 * D_CONV, D_SSM + 2 * D_CONV + D_MODEL]
    for l in range(DEPTH):
        h = rms_norm(h_res, norm_mix_g[l])
        proj = h @ w_in[l]
        u, cv, ga, gb = jnp.split(proj, split_pts, axis=-1)
        y = s5_mixer(u, lam_re[l], lam_im[l], log_dt[l], b_re[l], b_im[l], c_re[l], c_im[l], d_skip[l])
        z = jax.nn.gelu(y).astype(x.dtype)
        za, zb = jnp.split(z @ w_glu[l], 2, axis=-1)
        out_a = za * jax.nn.sigmoid(zb)
        c1, c2 = jnp.split(cv, 2, axis=-1)
        c = c1 * jax.nn.sigmoid(c2)
        c = causal_depthwise_conv(c, conv_w[l], conv_b[l])
        c = jax.nn.silu(layer_norm(c, conv_ln_g[l], conv_ln_b[l]))
        out_b = c @ w_conv_out[l]
        merged = jax.nn.sigmoid(ga) * out_a + jax.nn.sigmoid(gb) * out_b
        h_res = h_res + merged @ w_out[l]
        h = rms_norm(h_res, norm_ffn_g[l])
        h_res = h_res + jnp.square(jax.nn.relu(h @ w_ff1[l])) @ w_ff2[l]
    out = rms_norm(h_res, norm_f_g)
    return out[:, N_META:, :]
```

```python
import functools
import math

import jax
import jax.numpy as jnp
from jax import lax
from jax.experimental import pallas as pl
from jax.experimental.pallas import tpu as pltpu

F32 = jnp.float32
BF16 = jnp.bfloat16

D_MODEL = 2048
N_META = 16
D_SSM = 1024
SSM_GROUP = 16
N_GROUPS = 64
SSM_STATE = 64
D_CONV = 1024
CONV_WIDTH = 31
D_FF = 8192
EPS = 1e-6

S5_CHUNK = 16
S5_LANES = S5_CHUNK * SSM_GROUP
S5_GROUP_BLOCK = 8

VMEM_LIMIT = 56 * 1024 * 1024


def _cparams(sem):
    return pltpu.CompilerParams(dimension_semantics=sem, vmem_limit_bytes=VMEM_LIMIT)


def _dot(a, b):
    return jnp.dot(a, b, preferred_element_type=F32)


def _dot_nt(a, b, precision=None):
    return lax.dot_general(a, b, (((1,), (1,)), ((), ())),
                           preferred_element_type=F32, precision=precision)


def _rms(x, g):
    ms = jnp.mean(x * x, axis=-1, keepdims=True)
    return x * lax.rsqrt(ms + EPS) * g


def _rms_matmul_kernel(x_ref, g_ref, w_ref, o_ref, xn_ref):
    @pl.when(pl.program_id(1) == 0)
    def _():
        xn_ref[...] = _rms(x_ref[...], g_ref[...]).astype(BF16)

    o_ref[...] = _dot(xn_ref[...], w_ref[...])


def _rms_matmul(x, g, w, tm, tn):
    t, d = x.shape
    n = w.shape[1]
    return pl.pallas_call(
        _rms_matmul_kernel,
        grid=(t // tm, n // tn),
        in_specs=[pl.BlockSpec((tm, d), lambda i, j: (i, 0)),
                  pl.BlockSpec((1, d), lambda i, j: (0, 0)),
                  pl.BlockSpec((d, tn), lambda i, j: (0, j))],
        out_specs=pl.BlockSpec((tm, tn), lambda i, j: (i, j)),
        out_shape=jax.ShapeDtypeStruct((t, n), F32),
        scratch_shapes=[pltpu.VMEM((tm, d), BF16)],
        compiler_params=_cparams(("parallel", "arbitrary")),
        name="rms_in_proj",
    )(x, g, w)


def _s5_kernel(u_ref, lr_ref, li_ref, lrc_ref, lic_ref, ldt_ref, bre_ref, bim_ref,
               cre_ref, cim_ref, d_ref, y_ref,
               z_ref, qre_ref, qim_ref, xpre_ref, xpim_ref, pre_ref, pim_ref,
               *, n_chunks, n_batch):
    gb = S5_GROUP_BLOCK
    hi = lax.Precision.HIGHEST
    lane_blk = lax.broadcasted_iota(jnp.int32, (1, S5_LANES), 1) // SSM_GROUP
    kq = (S5_CHUNK - 1 - lane_blk).astype(F32)
    tau = (lax.broadcasted_iota(jnp.int32, ((S5_CHUNK + 1) * SSM_GROUP, 1), 0)
           // SSM_GROUP).astype(F32)
    z_ref[0:S5_LANES, :] = jnp.zeros((S5_LANES, S5_LANES), F32)

    for gi in range(gb):
        dt = jnp.exp(ldt_ref[gi])
        lrc = lrc_ref[gi]
        lic = lic_ref[gi]
        ar = lrc * dt
        ai = lic * dt
        e = jnp.exp(ar)
        lbr = e * jnp.cos(ai)
        lbi = e * jnp.sin(ai)
        den = lrc * lrc + lic * lic
        nr = lbr - 1.0
        fr = (nr * lrc + lbi * lic) / den
        fi = (lbi * lrc - nr * lic) / den
        bre = bre_ref[gi]
        bim = bim_ref[gi]
        bbr = fr * bre - fi * bim
        bbi = fr * bim + fi * bre
        pe = jnp.exp(ar * kq)
        pr = pe * jnp.cos(ai * kq)
        pi = pe * jnp.sin(ai * kq)
        qtr = pr * bbr - pi * bbi
        qti = pr * bbi + pi * bbr

        arr = lr_ref[gi] * dt
        air = li_ref[gi] * dt
        ce = jnp.exp(arr * tau)
        cpr = ce * jnp.cos(air * tau)
        cpi = ce * jnp.sin(air * tau)
        cre = cre_ref[gi]
        cim = cim_ref[gi]
        clr = cpr * cre - cpi * cim
        cli = cpr * cim + cpi * cre
        pre_ref[gi] = clr[SSM_GROUP:, :]
        pim_ref[gi] = cli[SSM_GROUP:, :]

        kt = (jnp.dot(clr[:S5_LANES], bbr, preferred_element_type=F32, precision=hi)
              - jnp.dot(cli[:S5_LANES], bbi, preferred_element_type=F32, precision=hi))
        z_ref[S5_LANES:, :] = kt
        mt = jnp.zeros((S5_LANES, S5_LANES), F32)
        for s in range(S5_CHUNK):
            lo = S5_LANES - SSM_GROUP * s
            mt = jnp.where(lane_blk == s, z_ref[lo:lo + S5_LANES, :], mt)

        u = u_ref[gi]
        y_ref[gi] = _dot_nt(u, mt, hi) + d_ref[gi] * u
        qre_ref[gi] = _dot_nt(u, qtr, hi)
        qim_ref[gi] = _dot_nt(u, qti, hi)

    dtb = jnp.exp(ldt_ref[...])
    a_e = jnp.exp(lr_ref[...] * dtb * float(S5_CHUNK))
    a_ph = li_ref[...] * dtb * float(S5_CHUNK)
    a_r = a_e * jnp.cos(a_ph)
    a_i = a_e * jnp.sin(a_ph)
    xr = jnp.zeros((gb, n_batch, SSM_STATE), F32)
    xi = jnp.zeros((gb, n_batch, SSM_STATE), F32)
    for c in range(n_chunks):
        rows = slice(c * n_batch, (c + 1) * n_batch)
        xpre_ref[:, rows, :] = xr
        xpim_ref[:, rows, :] = xi
        qr = qre_ref[:, rows, :]
        qi = qim_ref[:, rows, :]
        xr, xi = a_r * xr - a_i * xi + qr, a_r * xi + a_i * xr + qi
    n_rows = y_ref.shape[1]
    if n_rows > n_chunks * n_batch:
        pad = slice(n_chunks * n_batch, n_rows)
        xpre_ref[:, pad, :] = jnp.zeros((gb, n_rows - n_chunks * n_batch, SSM_STATE), F32)
        xpim_ref[:, pad, :] = jnp.zeros((gb, n_rows - n_chunks * n_batch, SSM_STATE), F32)

    for gi in range(gb):
        y_ref[gi] += (_dot_nt(xpre_ref[gi], pre_ref[gi], hi)
                      - _dot_nt(xpim_ref[gi], pim_ref[gi], hi))


def _s5_mixer(u_rows, lam_re, lam_im, log_dt, b_re, b_im, c_re, c_im, d_skip, n_chunks, n_batch):
    g, rows, _ = u_rows.shape
    gb = S5_GROUP_BLOCK
    lr = lam_re.reshape(g, 1, SSM_STATE)
    li = lam_im.reshape(g, 1, SSM_STATE)
    lrc = lam_re.reshape(g, SSM_STATE, 1)
    lic = lam_im.reshape(g, SSM_STATE, 1)
    ldt = log_dt.reshape(g, 1, 1)
    bre = jnp.tile(b_re, (1, 1, S5_CHUNK))
    bim = jnp.tile(b_im, (1, 1, S5_CHUNK))
    cre = jnp.tile(c_re, (1, S5_CHUNK + 1, 1))
    cim = jnp.tile(c_im, (1, S5_CHUNK + 1, 1))
    dsk = jnp.tile(d_skip.reshape(g, 1, SSM_GROUP), (1, 1, S5_CHUNK))

    def spec(shape):
        return pl.BlockSpec((gb,) + shape, lambda i: (i, 0, 0))

    kern = functools.partial(_s5_kernel, n_chunks=n_chunks, n_batch=n_batch)
    return pl.pallas_call(
        kern,
        grid=(g // gb,),
        in_specs=[spec((rows, S5_LANES)),
                  spec((1, SSM_STATE)), spec((1, SSM_STATE)),
                  spec((SSM_STATE, 1)), spec((SSM_STATE, 1)),
                  spec((1, 1)),
                  spec((SSM_STATE, S5_LANES)), spec((SSM_STATE, S5_LANES)),
                  spec(((S5_CHUNK + 1) * SSM_GROUP, SSM_STATE)),
                  spec(((S5_CHUNK + 1) * SSM_GROUP, SSM_STATE)),
                  spec((1, S5_LANES))],
        out_specs=spec((rows, S5_LANES)),
        out_shape=jax.ShapeDtypeStruct((g, rows, S5_LANES), F32),
        scratch_shapes=[pltpu.VMEM((2 * S5_LANES, S5_LANES), F32),
                        pltpu.VMEM((gb, rows, SSM_STATE), F32),
                        pltpu.VMEM((gb, rows, SSM_STATE), F32),
                        pltpu.VMEM((gb, rows, SSM_STATE), F32),
                        pltpu.VMEM((gb, rows, SSM_STATE), F32),
                        pltpu.VMEM((gb, S5_LANES, SSM_STATE), F32),
                        pltpu.VMEM((gb, S5_LANES, SSM_STATE), F32)],
        compiler_params=_cparams(("parallel",)),
        name="s5_mixer",
    )(u_rows, lr, li, lrc, lic, ldt, bre, bim, cre, cim, dsk)


def _gelu_tanh(x):
    c = math.sqrt(2.0 / math.pi)
    return x * (0.5 * (1.0 + jnp.tanh(c * (x + 0.044715 * (x * x * x)))))


def _glu_kernel(y_ref, wa_ref, wb_ref, o_ref, z_ref):
    @pl.when(pl.program_id(1) == 0)
    def _():
        z_ref[...] = _gelu_tanh(y_ref[...]).astype(BF16)

    z = z_ref[...]
    o_ref[...] = _dot(z, wa_ref[...]) * jax.nn.sigmoid(_dot(z, wb_ref[...]))


def _glu_proj(y, w_glu, tm, tn):
    t, k = y.shape
    n = w_glu.shape[1] // 2
    nb = n // tn
    return pl.pallas_call(
        _glu_kernel,
        grid=(t // tm, nb),
        in_specs=[pl.BlockSpec((tm, k), lambda i, j: (i, 0)),
                  pl.BlockSpec((k, tn), lambda i, j: (0, j)),
                  pl.BlockSpec((k, tn), lambda i, j: (0, j + nb))],
        out_specs=pl.BlockSpec((tm, tn), lambda i, j: (i, j)),
        out_shape=jax.ShapeDtypeStruct((t, n), F32),
        scratch_shapes=[pltpu.VMEM((tm, k), BF16)],
        compiler_params=_cparams(("parallel", "arbitrary")),
        name="s5_glu_proj",
    )(y, w_glu, w_glu)


CONV_HALO = 32
CONV_ROWS = 32
CONV_LANES = 256


def _conv_kernel(c1_ref, c2_ref, m1_ref, m2_ref, w_ref, b_ref, lg_ref, lb_ref, wo_ref, o_ref,
                 cbuf_ref, conv_ref, *, tr):
    r = pl.program_id(1)

    @pl.when(r == 0)
    def _():
        cbuf_ref[0:CONV_HALO - N_META, :] = jnp.zeros((CONV_HALO - N_META, D_CONV), F32)
        cbuf_ref[CONV_HALO - N_META:CONV_HALO, :] = m1_ref[...] * jax.nn.sigmoid(m2_ref[...])

    @pl.when(r > 0)
    def _():
        cbuf_ref[0:CONV_HALO, :] = cbuf_ref[tr:tr + CONV_HALO, :]

    cbuf_ref[CONV_HALO:, :] = c1_ref[...] * jax.nn.sigmoid(c2_ref[...])

    off = CONV_HALO - (CONV_WIDTH - 1)
    for rc in range(tr // CONV_ROWS):
        base = rc * CONV_ROWS
        for lc in range(D_CONV // CONV_LANES):
            lanes = slice(lc * CONV_LANES, (lc + 1) * CONV_LANES)
            acc = jnp.broadcast_to(b_ref[:, lanes], (CONV_ROWS, CONV_LANES))
            for k in range(CONV_WIDTH):
                acc = acc + w_ref[k:k + 1, lanes] * cbuf_ref[base + off + k:base + off + k + CONV_ROWS, lanes]
            conv_ref[base:base + CONV_ROWS, lanes] = acc

    v = conv_ref[...]
    mu = jnp.mean(v, axis=-1, keepdims=True)
    vc = v - mu
    var = jnp.mean(vc * vc, axis=-1, keepdims=True)
    yln = vc * lax.rsqrt(var + EPS) * lg_ref[...] + lb_ref[...]
    act = (yln * jax.nn.sigmoid(yln)).astype(BF16)
    o_ref[...] = _dot(act, wo_ref[...])


def _conv_branch(proj, proj_meta, conv_w, conv_b, ln_g, ln_b, w_out, n_batch, seq, tr):
    nr = seq // tr
    kern = functools.partial(_conv_kernel, tr=tr)
    vec = lambda: pl.BlockSpec((1, D_CONV), lambda b, r: (0, 0))
    return pl.pallas_call(
        kern,
        grid=(n_batch, nr),
        in_specs=[pl.BlockSpec((tr, D_CONV), lambda b, r: (b * nr + r, 1)),
                  pl.BlockSpec((tr, D_CONV), lambda b, r: (b * nr + r, 2)),
                  pl.BlockSpec((N_META, D_CONV), lambda b, r: (0, 1)),
                  pl.BlockSpec((N_META, D_CONV), lambda b, r: (0, 2)),
                  pl.BlockSpec((CONV_WIDTH, D_CONV), lambda b, r: (0, 0)),
                  vec(), vec(), vec(),
                  pl.BlockSpec((D_CONV, D_MODEL), lambda b, r: (0, 0))],
        out_specs=pl.BlockSpec((tr, D_MODEL), lambda b, r: (b * nr + r, 0)),
        out_shape=jax.ShapeDtypeStruct((n_batch * seq, D_MODEL), F32),
        scratch_shapes=[pltpu.VMEM((CONV_HALO + tr, D_CONV), F32),
                        pltpu.VMEM((tr, D_CONV), F32)],
        compiler_params=_cparams(("parallel", "arbitrary")),
        name="conv_branch",
    )(proj, proj, proj_meta, proj_meta, conv_w, conv_b, ln_g, ln_b, w_out)


def _merge_kernel(oa_ref, ob_ref, ga0_ref, ga1_ref, gb0_ref, gb1_ref, x_ref, w_ref, o_ref, m_ref):
    half = D_MODEL // 2

    @pl.when(pl.program_id(1) == 0)
    def _():
        for h, (ga_ref, gb_ref) in enumerate(((ga0_ref, gb0_ref), (ga1_ref, gb1_ref))):
            cols = slice(h * half, (h + 1) * half)
            m = (jax.nn.sigmoid(ga_ref[...]) * oa_ref[:, cols]
                 + jax.nn.sigmoid(gb_ref[...]) * ob_ref[:, cols])
            m_ref[:, cols] = m.astype(BF16)

    o_ref[...] = x_ref[...] + _dot(m_ref[...], w_ref[...])


def _merge_proj(out_a, out_b, proj, x, w_out, tm, tn):
    t, d = x.shape
    half = D_MODEL // 2
    gate = lambda blk: pl.BlockSpec((tm, half), lambda i, j: (i, blk))
    return pl.pallas_call(
        _merge_kernel,
        grid=(t // tm, d // tn),
        in_specs=[pl.BlockSpec((tm, d), lambda i, j: (i, 0)),
                  pl.BlockSpec((tm, d), lambda i, j: (i, 0)),
                  gate(3), gate(4), gate(5), gate(6),
                  pl.BlockSpec((tm, tn), lambda i, j: (i, j)),
                  pl.BlockSpec((d, tn), lambda i, j: (0, j))],
        out_specs=pl.BlockSpec((tm, tn), lambda i, j: (i, j)),
        out_shape=jax.ShapeDtypeStruct((t, d), F32),
        scratch_shapes=[pltpu.VMEM((tm, d), BF16)],
        compiler_params=_cparams(("parallel", "arbitrary")),
        name="merge_out_proj",
    )(out_a, out_b, proj, proj, proj, proj, x, w_out)


def _ffn_kernel(h_ref, g_ref, w1_ref, w2_ref, gf_ref, o_ref, xn_ref, acc_ref):
    f = pl.program_id(1)

    @pl.when(f == 0)
    def _():
        xn_ref[...] = _rms(h_ref[...], g_ref[...]).astype(BF16)
        acc_ref[...] = jnp.zeros_like(acc_ref)

    t = jnp.maximum(_dot(xn_ref[...], w1_ref[...]), 0.0)
    acc_ref[...] += _dot((t * t).astype(BF16), w2_ref[...])

    @pl.when(f == pl.num_programs(1) - 1)
    def _():
        o_ref[...] = _rms(h_ref[...] + acc_ref[...], gf_ref[...])


def _ffn(h, g, w1, w2, gf, tm, tf):
    t, d = h.shape
    ff = w1.shape[1]
    return pl.pallas_call(
        _ffn_kernel,
        grid=(t // tm, ff // tf),
        in_specs=[pl.BlockSpec((tm, d), lambda i, f: (i, 0)),
                  pl.BlockSpec((1, d), lambda i, f: (0, 0)),
                  pl.BlockSpec((d, tf), lambda i, f: (0, f)),
                  pl.BlockSpec((tf, d), lambda i, f: (f, 0)),
                  pl.BlockSpec((1, d), lambda i, f: (0, 0))],
        out_specs=pl.BlockSpec((tm, d), lambda i, f: (i, 0)),
        out_shape=jax.ShapeDtypeStruct((t, d), F32),
        scratch_shapes=[pltpu.VMEM((tm, d), BF16), pltpu.VMEM((tm, d), F32)],
        compiler_params=_cparams(("parallel", "arbitrary")),
        name="ffn_final_norm",
    )(h, g, w1, w2, gf)


def kernel(x, meta, norm_mix_g, w_in, lam_re, lam_im, log_dt, b_re, b_im, c_re, c_im, d_skip,
           w_glu, conv_w, conv_b, conv_ln_g, conv_ln_b, w_conv_out, w_out, norm_ffn_g,
           w_ff1, w_ff2, norm_f_g):
    bsz, seq, d = x.shape
    assert w_in.shape[0] == 1 and d == D_MODEL and meta.shape[0] == N_META
    assert seq % S5_CHUNK == 0
    xr = x.reshape(bsz * seq, d)
    g_mix = norm_mix_g[0].reshape(1, d)
    w_in_b = w_in[0].astype(BF16)

    proj = _rms_matmul(xr, g_mix, w_in_b, 512, 1024)
    proj_meta = _rms_matmul(meta, g_mix, w_in_b, N_META, 1024)

    nc = seq // S5_CHUNK
    u_real = proj[:, :D_SSM].reshape(bsz, nc, S5_CHUNK, N_GROUPS, SSM_GROUP)
    u_meta = jnp.broadcast_to(
        proj_meta[:, :D_SSM].reshape(1, 1, S5_CHUNK, N_GROUPS, SSM_GROUP),
        (bsz, 1, S5_CHUNK, N_GROUPS, SSM_GROUP))
    u_all = jnp.concatenate([u_meta, u_real], axis=1)
    n_chunks = nc + 1
    rows = n_chunks * bsz
    rows_pad = -(-rows // 8) * 8
    u_rows = u_all.transpose(3, 1, 0, 2, 4).reshape(N_GROUPS, rows, S5_LANES)
    u_rows = jnp.pad(u_rows, ((0, 0), (0, rows_pad - rows), (0, 0)))
    y_rows = _s5_mixer(u_rows, lam_re[0], lam_im[0], log_dt[0], b_re[0], b_im[0],
                       c_re[0], c_im[0], d_skip[0], n_chunks, bsz)
    y = y_rows[:, bsz:rows, :].reshape(N_GROUPS, nc, bsz, S5_CHUNK, SSM_GROUP)
    y = y.transpose(2, 1, 3, 0, 4).reshape(bsz * seq, D_SSM)

    out_a = _glu_proj(y, w_glu[0].astype(BF16), 512, 1024)

    out_b = _conv_branch(proj, proj_meta, conv_w[0], conv_b[0].reshape(1, -1),
                         conv_ln_g[0].reshape(1, -1), conv_ln_b[0].reshape(1, -1),
                         w_conv_out[0].astype(BF16), bsz, seq, 512)

    h1 = _merge_proj(out_a, out_b, proj, xr, w_out[0].astype(BF16), 512, 1024)

    out = _ffn(h1, norm_ffn_g[0].reshape(1, d), w_ff1[0].astype(BF16), w_ff2[0].astype(BF16),
               norm_f_g.reshape(1, d), 512, 1024)
    return out.reshape(bsz, seq, d)
```

```python
import functools
import math

import jax
import jax.numpy as jnp
from jax import lax
from jax.experimental import pallas as pl
from jax.experimental.pallas import tpu as pltpu

F32 = jnp.float32
BF16 = jnp.bfloat16

D_MODEL = 2048
N_META = 16
D_SSM = 1024
SSM_GROUP = 16
N_GROUPS = 64
SSM_STATE = 64
D_CONV = 1024
CONV_WIDTH = 31
D_FF = 8192
EPS = 1e-6
LANES = 128

S5_CHUNK = 16
S5_ROWS = S5_CHUNK * SSM_GROUP
S5_GROUP_BLOCK = LANES // SSM_GROUP

VMEM_LIMIT = 56 * 1024 * 1024


def _cparams(sem):
    return pltpu.CompilerParams(dimension_semantics=sem, vmem_limit_bytes=VMEM_LIMIT)


def _dot(a, b, precision=None):
    return jnp.dot(a, b, preferred_element_type=F32, precision=precision)


def _rms(x, g):
    ms = jnp.mean(x * x, axis=-1, keepdims=True)
    return x * lax.rsqrt(ms + EPS) * g


def _rms_matmul_kernel(x_ref, g_ref, w_ref, o_ref, xn_ref):
    @pl.when(pl.program_id(1) == 0)
    def _():
        xn_ref[...] = _rms(x_ref[...], g_ref[...]).astype(BF16)

    o_ref[...] = _dot(xn_ref[...], w_ref[...])


def _rms_matmul(x, g, w, tm, tn):
    t, d = x.shape
    n = w.shape[1]
    return pl.pallas_call(
        _rms_matmul_kernel,
        grid=(t // tm, n // tn),
        in_specs=[pl.BlockSpec((tm, d), lambda i, j: (i, 0)),
                  pl.BlockSpec((1, d), lambda i, j: (0, 0)),
                  pl.BlockSpec((d, tn), lambda i, j: (0, j))],
        out_specs=pl.BlockSpec((tm, tn), lambda i, j: (i, j)),
        out_shape=jax.ShapeDtypeStruct((t, n), F32),
        scratch_shapes=[pltpu.VMEM((tm, d), BF16)],
        compiler_params=_cparams(("parallel", "arbitrary")),
        name="rms_in_proj",
    )(x, g, w)


def _cmul(ar, ai, br, bi):
    return ar * br - ai * bi, ar * bi + ai * br


def _s5_kernel(u_ref, um_ref, lr_ref, li_ref, lrc_ref, lic_ref, ldt_ref, bre_ref, bim_ref,
               cre_ref, cim_ref, d_ref, y_ref, ut_ref, yt_ref, z_ref, *, n_batch, n_chunks):
    gb = S5_GROUP_BLOCK
    hi = lax.Precision.HIGHEST
    n_lanes = n_batch * n_chunks

    for b in range(n_batch):
        for s in range(S5_CHUNK):
            piece = u_ref[b, pl.ds(s, n_chunks, stride=S5_CHUNK), :]
            pt = piece.T
            for g in range(gb):
                ut_ref[g, s * SSM_GROUP:(s + 1) * SSM_GROUP, b * n_chunks:(b + 1) * n_chunks] = (
                    pt[g * SSM_GROUP:(g + 1) * SSM_GROUP, :])

    lane_blk = lax.broadcasted_iota(jnp.int32, (1, S5_ROWS), 1) // SSM_GROUP
    kq = (S5_CHUNK - 1 - lane_blk).astype(F32)
    tau = (lax.broadcasted_iota(jnp.int32, (S5_ROWS + SSM_GROUP, 1), 0)
           // SSM_GROUP).astype(F32)
    lane_c = lax.broadcasted_iota(jnp.int32, (1, n_lanes), 1) % n_chunks
    first = lane_c == 0
    z_ref[0:S5_ROWS, :] = jnp.zeros((S5_ROWS, S5_ROWS), F32)

    for gi in range(gb):
        dt = jnp.exp(ldt_ref[gi])
        lrc = lrc_ref[gi]
        lic = lic_ref[gi]
        ar = lrc * dt
        ai = lic * dt
        e = jnp.exp(ar)
        lbr = e * jnp.cos(ai)
        lbi = e * jnp.sin(ai)
        den = lrc * lrc + lic * lic
        nr = lbr - 1.0
        fr = (nr * lrc + lbi * lic) / den
        fi = (lbi * lrc - nr * lic) / den
        bbr, bbi = _cmul(fr, fi, bre_ref[gi], bim_ref[gi])
        pe = jnp.exp(ar * kq)
        qtr, qti = _cmul(pe * jnp.cos(ai * kq), pe * jnp.sin(ai * kq), bbr, bbi)

        arr = lr_ref[gi] * dt
        air = li_ref[gi] * dt
        ce = jnp.exp(arr * tau)
        clr, cli = _cmul(ce * jnp.cos(air * tau), ce * jnp.sin(air * tau),
                         cre_ref[gi], cim_ref[gi])

        kt = _dot(clr[:S5_ROWS], bbr, hi) - _dot(cli[:S5_ROWS], bbi, hi)
        z_ref[S5_ROWS:, :] = kt
        mt = jnp.zeros((S5_ROWS, S5_ROWS), F32)
        for s in range(S5_CHUNK):
            lo = S5_ROWS - SSM_GROUP * s
            mt = jnp.where(lane_blk == s, z_ref[lo:lo + S5_ROWS, :], mt)

        ut = ut_ref[gi]
        utb = ut.astype(BF16)
        yt_ref[gi] = _dot(mt.astype(BF16), utb) + d_ref[gi] * ut

        um = um_ref[gi]
        x0r = jnp.sum(qtr * um, axis=1, keepdims=True)
        x0i = jnp.sum(qti * um, axis=1, keepdims=True)
        pr, pi = lbr, lbi
        for _ in range(4):
            pr, pi = _cmul(pr, pi, pr, pi)
        axr, axi = _cmul(pr, pi, x0r, x0i)
        qr = _dot(qtr.astype(BF16), utb)
        qi = _dot(qti.astype(BF16), utb)
        qr = jnp.where(first, qr + axr, qr)
        qi = jnp.where(first, qi + axi, qi)
        sh = 1
        while sh < n_chunks:
            rr = pltpu.roll(qr, sh, 1)
            ri = pltpu.roll(qi, sh, 1)
            tr, ti = _cmul(pr, pi, rr, ri)
            keep = lane_c >= sh
            qr = qr + jnp.where(keep, tr, 0.0)
            qi = qi + jnp.where(keep, ti, 0.0)
            pr, pi = _cmul(pr, pi, pr, pi)
            sh *= 2
        xpr = jnp.where(first, x0r, pltpu.roll(qr, 1, 1))
        xpi = jnp.where(first, x0i, pltpu.roll(qi, 1, 1))
        yt_ref[gi] += (_dot(clr[SSM_GROUP:].astype(BF16), xpr.astype(BF16))
                       - _dot(cli[SSM_GROUP:].astype(BF16), xpi.astype(BF16)))

    for b in range(n_batch):
        for t in range(S5_CHUNK):
            piece = jnp.concatenate(
                [yt_ref[g, t * SSM_GROUP:(t + 1) * SSM_GROUP, b * n_chunks:(b + 1) * n_chunks]
                 for g in range(gb)], axis=0)
            y_ref[b, pl.ds(t, n_chunks, stride=S5_CHUNK), :] = piece.T


def _s5_mixer(proj3, u_meta, lam_re, lam_im, log_dt, b_re, b_im, c_re, c_im, d_skip):
    n_batch, seq, _ = proj3.shape
    n_chunks = seq // S5_CHUNK
    assert n_chunks == LANES
    g = N_GROUPS
    gb = S5_GROUP_BLOCK
    um = u_meta.reshape(S5_CHUNK, g, SSM_GROUP).transpose(1, 0, 2).reshape(g, 1, S5_ROWS)
    lr = lam_re.reshape(g, 1, SSM_STATE)
    li = lam_im.reshape(g, 1, SSM_STATE)
    lrc = lam_re.reshape(g, SSM_STATE, 1)
    lic = lam_im.reshape(g, SSM_STATE, 1)
    ldt = log_dt.reshape(g, 1, 1)
    bre = jnp.tile(b_re, (1, 1, S5_CHUNK))
    bim = jnp.tile(b_im, (1, 1, S5_CHUNK))
    cre = jnp.tile(c_re, (1, S5_CHUNK + 1, 1))
    cim = jnp.tile(c_im, (1, S5_CHUNK + 1, 1))
    dsk = jnp.tile(d_skip.reshape(g, SSM_GROUP, 1), (1, S5_CHUNK, 1))

    def spec(shape):
        return pl.BlockSpec((gb,) + shape, lambda i: (i, 0, 0))

    slab = pl.BlockSpec((n_batch, seq, LANES), lambda i: (0, 0, i))
    kern = functools.partial(_s5_kernel, n_batch=n_batch, n_chunks=n_chunks)
    return pl.pallas_call(
        kern,
        grid=(g // gb,),
        in_specs=[slab,
                  spec((1, S5_ROWS)),
                  spec((1, SSM_STATE)), spec((1, SSM_STATE)),
                  spec((SSM_STATE, 1)), spec((SSM_STATE, 1)),
                  spec((1, 1)),
                  spec((SSM_STATE, S5_ROWS)), spec((SSM_STATE, S5_ROWS)),
                  spec((S5_ROWS + SSM_GROUP, SSM_STATE)),
                  spec((S5_ROWS + SSM_GROUP, SSM_STATE)),
                  spec((S5_ROWS, 1))],
        out_specs=slab,
        out_shape=jax.ShapeDtypeStruct((n_batch, seq, D_SSM), F32),
        scratch_shapes=[pltpu.VMEM((gb, S5_ROWS, n_batch * n_chunks), F32),
                        pltpu.VMEM((gb, S5_ROWS, n_batch * n_chunks), F32),
                        pltpu.VMEM((2 * S5_ROWS, S5_ROWS), F32)],
        compiler_params=_cparams(("parallel",)),
        name="s5_mixer",
    )(proj3, um, lr, li, lrc, lic, ldt, bre, bim, cre, cim, dsk)


def _gelu_tanh(x):
    c = math.sqrt(2.0 / math.pi)
    return x * (0.5 * (1.0 + jnp.tanh(c * (x + 0.044715 * (x * x * x)))))


def _glu_kernel(y_ref, wa_ref, wb_ref, o_ref, z_ref):
    @pl.when(pl.program_id(1) == 0)
    def _():
        z_ref[...] = _gelu_tanh(y_ref[...]).astype(BF16)

    z = z_ref[...]
    o_ref[...] = _dot(z, wa_ref[...]) * jax.nn.sigmoid(_dot(z, wb_ref[...]))


def _glu_proj(y, w_glu, tm, tn):
    t, k = y.shape
    n = w_glu.shape[1] // 2
    nb = n // tn
    return pl.pallas_call(
        _glu_kernel,
        grid=(t // tm, nb),
        in_specs=[pl.BlockSpec((tm, k), lambda i, j: (i, 0)),
                  pl.BlockSpec((k, tn), lambda i, j: (0, j)),
                  pl.BlockSpec((k, tn), lambda i, j: (0, j + nb))],
        out_specs=pl.BlockSpec((tm, tn), lambda i, j: (i, j)),
        out_shape=jax.ShapeDtypeStruct((t, n), F32),
        scratch_shapes=[pltpu.VMEM((tm, k), BF16)],
        compiler_params=_cparams(("parallel", "arbitrary")),
        name="s5_glu_proj",
    )(y, w_glu, w_glu)


CONV_HALO = 32
CONV_ROWS = 64
CONV_SLABS = D_CONV // LANES


def _conv_kernel(c1_ref, c2_ref, m1_ref, m2_ref, w_ref, b_ref, lg_ref, lb_ref, wo_ref, o_ref,
                 cbuf_ref, conv_ref, act_ref, *, tr):
    r = pl.program_id(1)

    @pl.when(r == 0)
    def _():
        cm = m1_ref[...] * jax.nn.sigmoid(m2_ref[...])
        for j in range(CONV_SLABS):
            cbuf_ref[j, 0:CONV_HALO - N_META, :] = jnp.zeros((CONV_HALO - N_META, LANES), F32)
            cbuf_ref[j, CONV_HALO - N_META:CONV_HALO, :] = cm[:, j * LANES:(j + 1) * LANES]

    @pl.when(r > 0)
    def _():
        cbuf_ref[:, 0:CONV_HALO, :] = cbuf_ref[:, tr:tr + CONV_HALO, :]

    c = c1_ref[...] * jax.nn.sigmoid(c2_ref[...])
    for j in range(CONV_SLABS):
        cbuf_ref[j, CONV_HALO:, :] = c[:, j * LANES:(j + 1) * LANES]

    off = CONV_HALO - (CONV_WIDTH - 1)
    half = CONV_ROWS // 2

    def chunk(rc, carry):
        base = pl.multiple_of(rc * CONV_ROWS, CONV_ROWS)
        for j in range(CONV_SLABS):
            lanes = slice(j * LANES, (j + 1) * LANES)
            bias = jnp.broadcast_to(b_ref[:, lanes], (half, LANES))
            acc_e = bias
            acc_o = bias
            for k in range(CONV_WIDTH):
                wk = w_ref[k:k + 1, lanes]
                acc_e = acc_e + wk * cbuf_ref[j, pl.ds(base + off + k, half, stride=2), :]
                acc_o = acc_o + wk * cbuf_ref[j, pl.ds(base + off + k + 1, half, stride=2), :]
            conv_ref[j, pl.ds(base, half, stride=2), :] = acc_e
            conv_ref[j, pl.ds(base + 1, half, stride=2), :] = acc_o
        return carry

    lax.fori_loop(0, tr // CONV_ROWS, chunk, 0)

    s1 = jnp.zeros((tr, 1), F32)
    for j in range(CONV_SLABS):
        s1 = s1 + jnp.sum(conv_ref[j], axis=-1, keepdims=True)
    mu = s1 * (1.0 / D_CONV)
    s2 = jnp.zeros((tr, 1), F32)
    for j in range(CONV_SLABS):
        vc = conv_ref[j] - mu
        s2 = s2 + jnp.sum(vc * vc, axis=-1, keepdims=True)
    inv = lax.rsqrt(s2 * (1.0 / D_CONV) + EPS)
    for j in range(CONV_SLABS):
        lanes = slice(j * LANES, (j + 1) * LANES)
        yln = (conv_ref[j] - mu) * inv * lg_ref[:, lanes] + lb_ref[:, lanes]
        act_ref[:, lanes] = (yln * jax.nn.sigmoid(yln)).astype(BF16)
    o_ref[...] = _dot(act_ref[...], wo_ref[...])


def _conv_branch(proj, proj_meta, conv_w, conv_b, ln_g, ln_b, w_out, n_batch, seq, tr):
    nr = seq // tr
    kern = functools.partial(_conv_kernel, tr=tr)
    vec = lambda: pl.BlockSpec((1, D_CONV), lambda b, r: (0, 0))
    return pl.pallas_call(
        kern,
        grid=(n_batch, nr),
        in_specs=[pl.BlockSpec((tr, D_CONV), lambda b, r: (b * nr + r, 1)),
                  pl.BlockSpec((tr, D_CONV), lambda b, r: (b * nr + r, 2)),
                  pl.BlockSpec((N_META, D_CONV), lambda b, r: (0, 1)),
                  pl.BlockSpec((N_META, D_CONV), lambda b, r: (0, 2)),
                  pl.BlockSpec((CONV_WIDTH, D_CONV), lambda b, r: (0, 0)),
                  vec(), vec(), vec(),
                  pl.BlockSpec((D_CONV, D_MODEL), lambda b, r: (0, 0))],
        out_specs=pl.BlockSpec((tr, D_MODEL), lambda b, r: (b * nr + r, 0)),
        out_shape=jax.ShapeDtypeStruct((n_batch * seq, D_MODEL), F32),
        scratch_shapes=[pltpu.VMEM((CONV_SLABS, CONV_HALO + tr, LANES), F32),
                        pltpu.VMEM((CONV_SLABS, tr, LANES), F32),
                        pltpu.VMEM((tr, D_CONV), BF16)],
        compiler_params=_cparams(("parallel", "arbitrary")),
        name="conv_branch",
    )(proj, proj, proj_meta, proj_meta, conv_w, conv_b, ln_g, ln_b, w_out)


def _merge_kernel(oa_ref, ob_ref, ga0_ref, ga1_ref, gb0_ref, gb1_ref, x_ref, w_ref, o_ref, m_ref):
    half = D_MODEL // 2

    @pl.when(pl.program_id(1) == 0)
    def _():
        for h, (ga_ref, gb_ref) in enumerate(((ga0_ref, gb0_ref), (ga1_ref, gb1_ref))):
            cols = slice(h * half, (h + 1) * half)
            m = (jax.nn.sigmoid(ga_ref[...]) * oa_ref[:, cols]
                 + jax.nn.sigmoid(gb_ref[...]) * ob_ref[:, cols])
            m_ref[:, cols] = m.astype(BF16)

    o_ref[...] = x_ref[...] + _dot(m_ref[...], w_ref[...])


def _merge_proj(out_a, out_b, proj, x, w_out, tm, tn):
    t, d = x.shape
    half = D_MODEL // 2
    gate = lambda blk: pl.BlockSpec((tm, half), lambda i, j: (i, blk))
    return pl.pallas_call(
        _merge_kernel,
        grid=(t // tm, d // tn),
        in_specs=[pl.BlockSpec((tm, d), lambda i, j: (i, 0)),
                  pl.BlockSpec((tm, d), lambda i, j: (i, 0)),
                  gate(3), gate(4), gate(5), gate(6),
                  pl.BlockSpec((tm, tn), lambda i, j: (i, j)),
                  pl.BlockSpec((d, tn), lambda i, j: (0, j))],
        out_specs=pl.BlockSpec((tm, tn), lambda i, j: (i, j)),
        out_shape=jax.ShapeDtypeStruct((t, d), F32),
        scratch_shapes=[pltpu.VMEM((tm, d), BF16)],
        compiler_params=_cparams(("parallel", "arbitrary")),
        name="merge_out_proj",
    )(out_a, out_b, proj, proj, proj, proj, x, w_out)


def _ffn_kernel(h_ref, g_ref, w1_ref, w2_ref, gf_ref, o_ref, xn_ref, acc_ref):
    f = pl.program_id(1)

    @pl.when(f == 0)
    def _():
        xn_ref[...] = _rms(h_ref[...], g_ref[...]).astype(BF16)
        acc_ref[...] = jnp.zeros_like(acc_ref)

    t = jnp.maximum(_dot(xn_ref[...], w1_ref[...]), 0.0)
    acc_ref[...] += _dot((t * t).astype(BF16), w2_ref[...])

    @pl.when(f == pl.num_programs(1) - 1)
    def _():
        o_ref[...] = _rms(h_ref[...] + acc_ref[...], gf_ref[...])


def _ffn(h, g, w1, w2, gf, tm, tf):
    t, d = h.shape
    ff = w1.shape[1]
    return pl.pallas_call(
        _ffn_kernel,
        grid=(t // tm, ff // tf),
        in_specs=[pl.BlockSpec((tm, d), lambda i, f: (i, 0)),
                  pl.BlockSpec((1, d), lambda i, f: (0, 0)),
                  pl.BlockSpec((d, tf), lambda i, f: (0, f)),
                  pl.BlockSpec((tf, d), lambda i, f: (f, 0)),
                  pl.BlockSpec((1, d), lambda i, f: (0, 0))],
        out_specs=pl.BlockSpec((tm, d), lambda i, f: (i, 0)),
        out_shape=jax.ShapeDtypeStruct((t, d), F32),
        scratch_shapes=[pltpu.VMEM((tm, d), BF16), pltpu.VMEM((tm, d), F32)],
        compiler_params=_cparams(("parallel", "arbitrary")),
        name="ffn_final_norm",
    )(h, g, w1, w2, gf)


def kernel(x, meta, norm_mix_g, w_in, lam_re, lam_im, log_dt, b_re, b_im, c_re, c_im, d_skip,
           w_glu, conv_w, conv_b, conv_ln_g, conv_ln_b, w_conv_out, w_out, norm_ffn_g,
           w_ff1, w_ff2, norm_f_g):
    bsz, seq, d = x.shape
    assert w_in.shape[0] == 1 and d == D_MODEL and meta.shape[0] == N_META
    xr = x.reshape(bsz * seq, d)
    g_mix = norm_mix_g[0].reshape(1, d)
    w_in_b = w_in[0].astype(BF16)

    proj = _rms_matmul(xr, g_mix, w_in_b, 512, 1024)
    proj_meta = _rms_matmul(meta, g_mix, w_in_b, N_META, 1024)

    y = _s5_mixer(proj.reshape(bsz, seq, -1), proj_meta[:, :D_SSM], lam_re[0], lam_im[0],
                  log_dt[0], b_re[0], b_im[0], c_re[0], c_im[0], d_skip[0])
    y = y.reshape(bsz * seq, D_SSM)

    out_a = _glu_proj(y, w_glu[0].astype(BF16), 512, 1024)

    out_b = _conv_branch(proj, proj_meta, conv_w[0], conv_b[0].reshape(1, -1),
                         conv_ln_g[0].reshape(1, -1), conv_ln_b[0].reshape(1, -1),
                         w_conv_out[0].astype(BF16), bsz, seq, 512)

    h1 = _merge_proj(out_a, out_b, proj, xr, w_out[0].astype(BF16), 512, 1024)

    out = _ffn(h1, norm_ffn_g[0].reshape(1, d), w_ff1[0].astype(BF16), w_ff2[0].astype(BF16),
               norm_f_g.reshape(1, d), 512, 1024)
    return out.reshape(bsz, seq, d)
```

```python
import functools
import math

import jax
import jax.numpy as jnp
from jax import lax
from jax.experimental import pallas as pl
from jax.experimental.pallas import tpu as pltpu

F32 = jnp.float32
BF16 = jnp.bfloat16

D_MODEL = 2048
N_META = 16
D_SSM = 1024
SSM_GROUP = 16
N_GROUPS = 64
SSM_STATE = 64
D_CONV = 1024
CONV_WIDTH = 31
D_FF = 8192
EPS = 1e-6
LANES = 128

S5_CHUNK = 16
S5_ROWS = S5_CHUNK * SSM_GROUP
S5_GROUP_BLOCK = LANES // SSM_GROUP

VMEM_LIMIT = 56 * 1024 * 1024


def _cparams(sem):
    return pltpu.CompilerParams(dimension_semantics=sem, vmem_limit_bytes=VMEM_LIMIT)


def _dot(a, b, precision=None):
    return jnp.dot(a, b, preferred_element_type=F32, precision=precision)


def _rms(x, g):
    ms = jnp.mean(x * x, axis=-1, keepdims=True)
    return x * lax.rsqrt(ms + EPS) * g


def _rms_matmul_kernel(x_ref, g_ref, w_ref, o_ref, s_ref, xn_ref, *, n_plain):
    j = pl.program_id(1)

    @pl.when(j == 0)
    def _():
        xn_ref[...] = _rms(x_ref[...], g_ref[...]).astype(BF16)

    acc = _dot(xn_ref[...], w_ref[...])

    @pl.when(j < n_plain)
    def _():
        o_ref[...] = acc

    @pl.when(j >= n_plain)
    def _():
        s_ref[...] = jax.nn.sigmoid(acc).astype(BF16)


def _rms_matmul(x, g, w, tm, tn, n_plain_cols):
    t, d = x.shape
    n = w.shape[1]
    n_plain = n_plain_cols // tn
    kern = functools.partial(_rms_matmul_kernel, n_plain=n_plain)
    return pl.pallas_call(
        kern,
        grid=(t // tm, n // tn),
        in_specs=[pl.BlockSpec((tm, d), lambda i, j: (i, 0)),
                  pl.BlockSpec((1, d), lambda i, j: (0, 0)),
                  pl.BlockSpec((d, tn), lambda i, j: (0, j))],
        out_specs=[pl.BlockSpec((tm, tn), lambda i, j: (i, jnp.minimum(j, n_plain - 1))),
                   pl.BlockSpec((tm, tn), lambda i, j: (i, jnp.maximum(j - n_plain, 0)))],
        out_shape=[jax.ShapeDtypeStruct((t, n_plain_cols), F32),
                   jax.ShapeDtypeStruct((t, n - n_plain_cols), BF16)],
        scratch_shapes=[pltpu.VMEM((tm, d), BF16)],
        compiler_params=_cparams(("parallel", "arbitrary")),
        name="rms_in_proj",
    )(x, g, w)


def _cmul(ar, ai, br, bi):
    return ar * br - ai * bi, ar * bi + ai * br


def _cpow_table(br, bi, k, n_bits):
    bit = (k & 1) > 0
    pr = jnp.where(bit, br, 1.0)
    pi = jnp.where(bit, bi, 0.0)
    for j in range(1, n_bits):
        br, bi = _cmul(br, bi, br, bi)
        tr, ti = _cmul(pr, pi, br, bi)
        bit = ((k >> j) & 1) > 0
        pr = jnp.where(bit, tr, pr)
        pi = jnp.where(bit, ti, pi)
    return pr, pi


def _dot3(a, b):
    ah = a.astype(BF16)
    al = (a - ah.astype(F32)).astype(BF16)
    bh = b.astype(BF16)
    bl = (b - bh.astype(F32)).astype(BF16)
    return _dot(ah, bh) + (_dot(ah, bl) + _dot(al, bh))


def _s5_kernel(u_ref, um_ref, lr_ref, li_ref, lrc_ref, lic_ref, ldt_ref, bre_ref, bim_ref,
               cre_ref, cim_ref, d_ref, y_ref, ut_ref, yt_ref, z_ref, *, n_batch, n_chunks):
    gb = S5_GROUP_BLOCK
    n_lanes = n_batch * n_chunks

    for b in range(n_batch):
        for s in range(S5_CHUNK):
            piece = u_ref[b, pl.ds(s, n_chunks, stride=S5_CHUNK), :]
            pt = piece.T
            for g in range(gb):
                ut_ref[g, s * SSM_GROUP:(s + 1) * SSM_GROUP, b * n_chunks:(b + 1) * n_chunks] = (
                    pt[g * SSM_GROUP:(g + 1) * SSM_GROUP, :])

    lane_blk = lax.broadcasted_iota(jnp.int32, (1, S5_ROWS), 1) // SSM_GROUP
    kq = S5_CHUNK - 1 - lane_blk
    tau = lax.broadcasted_iota(jnp.int32, (S5_ROWS + SSM_GROUP, 1), 0) // SSM_GROUP
    lane_c = lax.broadcasted_iota(jnp.int32, (1, n_lanes), 1) % n_chunks
    first = lane_c == 0
    z_ref[0:S5_ROWS, :] = jnp.zeros((S5_ROWS, S5_ROWS), F32)

    for gi in range(gb):
        dt = jnp.exp(ldt_ref[gi])
        lrc = lrc_ref[gi]
        lic = lic_ref[gi]
        ar = lrc * dt
        ai = lic * dt
        e = jnp.exp(ar)
        lbr = e * jnp.cos(ai)
        lbi = e * jnp.sin(ai)
        den = lrc * lrc + lic * lic
        nr = lbr - 1.0
        fr = (nr * lrc + lbi * lic) / den
        fi = (lbi * lrc - nr * lic) / den
        bbr, bbi = _cmul(fr, fi, bre_ref[gi], bim_ref[gi])
        qpr, qpi = _cpow_table(lbr, lbi, kq, 4)
        qtr, qti = _cmul(qpr, qpi, bbr, bbi)

        er = jnp.exp(lr_ref[gi] * dt)
        air = li_ref[gi] * dt
        cpr, cpi = _cpow_table(er * jnp.cos(air), er * jnp.sin(air), tau, 5)
        clr, cli = _cmul(cpr, cpi, cre_ref[gi], cim_ref[gi])

        kt = _dot3(clr[:S5_ROWS], bbr) - _dot3(cli[:S5_ROWS], bbi)
        z_ref[S5_ROWS:, :] = kt
        mt = jnp.zeros((S5_ROWS, S5_ROWS), F32)
        for s in range(S5_CHUNK):
            lo = S5_ROWS - SSM_GROUP * s
            mt = jnp.where(lane_blk == s, z_ref[lo:lo + S5_ROWS, :], mt)

        ut = ut_ref[gi]
        utb = ut.astype(BF16)
        yt_ref[gi] = _dot(mt.astype(BF16), utb) + d_ref[gi] * ut

        um = um_ref[gi]
        x0r = jnp.sum(qtr * um, axis=1, keepdims=True)
        x0i = jnp.sum(qti * um, axis=1, keepdims=True)
        pr, pi = lbr, lbi
        for _ in range(4):
            pr, pi = _cmul(pr, pi, pr, pi)
        axr, axi = _cmul(pr, pi, x0r, x0i)
        qr = _dot(qtr.astype(BF16), utb)
        qi = _dot(qti.astype(BF16), utb)
        qr = jnp.where(first, qr + axr, qr)
        qi = jnp.where(first, qi + axi, qi)
        sh = 1
        while sh < n_chunks:
            rr = pltpu.roll(qr, sh, 1)
            ri = pltpu.roll(qi, sh, 1)
            tr, ti = _cmul(pr, pi, rr, ri)
            keep = lane_c >= sh
            qr = qr + jnp.where(keep, tr, 0.0)
            qi = qi + jnp.where(keep, ti, 0.0)
            pr, pi = _cmul(pr, pi, pr, pi)
            sh *= 2
        xpr = jnp.where(first, x0r, pltpu.roll(qr, 1, 1))
        xpi = jnp.where(first, x0i, pltpu.roll(qi, 1, 1))
        yt_ref[gi] += (_dot(clr[SSM_GROUP:].astype(BF16), xpr.astype(BF16))
                       - _dot(cli[SSM_GROUP:].astype(BF16), xpi.astype(BF16)))

    for b in range(n_batch):
        for t in range(S5_CHUNK):
            piece = jnp.concatenate(
                [yt_ref[g, t * SSM_GROUP:(t + 1) * SSM_GROUP, b * n_chunks:(b + 1) * n_chunks]
                 for g in range(gb)], axis=0)
            y_ref[b, pl.ds(t, n_chunks, stride=S5_CHUNK), :] = piece.T


def _s5_mixer(proj3, u_meta, lam_re, lam_im, log_dt, b_re, b_im, c_re, c_im, d_skip):
    n_batch, seq, _ = proj3.shape
    n_chunks = seq // S5_CHUNK
    assert n_chunks == LANES
    g = N_GROUPS
    gb = S5_GROUP_BLOCK
    um = u_meta.reshape(S5_CHUNK, g, SSM_GROUP).transpose(1, 0, 2).reshape(g, 1, S5_ROWS)
    lr = lam_re.reshape(g, 1, SSM_STATE)
    li = lam_im.reshape(g, 1, SSM_STATE)
    lrc = lam_re.reshape(g, SSM_STATE, 1)
    lic = lam_im.reshape(g, SSM_STATE, 1)
    ldt = log_dt.reshape(g, 1, 1)
    bre = jnp.tile(b_re, (1, 1, S5_CHUNK))
    bim = jnp.tile(b_im, (1, 1, S5_CHUNK))
    cre = jnp.tile(c_re, (1, S5_CHUNK + 1, 1))
    cim = jnp.tile(c_im, (1, S5_CHUNK + 1, 1))
    dsk = jnp.tile(d_skip.reshape(g, SSM_GROUP, 1), (1, S5_CHUNK, 1))

    def spec(shape):
        return pl.BlockSpec((gb,) + shape, lambda i: (i, 0, 0))

    slab = pl.BlockSpec((n_batch, seq, LANES), lambda i: (0, 0, i))
    kern = functools.partial(_s5_kernel, n_batch=n_batch, n_chunks=n_chunks)
    return pl.pallas_call(
        kern,
        grid=(g // gb,),
        in_specs=[slab,
                  spec((1, S5_ROWS)),
                  spec((1, SSM_STATE)), spec((1, SSM_STATE)),
                  spec((SSM_STATE, 1)), spec((SSM_STATE, 1)),
                  spec((1, 1)),
                  spec((SSM_STATE, S5_ROWS)), spec((SSM_STATE, S5_ROWS)),
                  spec((S5_ROWS + SSM_GROUP, SSM_STATE)),
                  spec((S5_ROWS + SSM_GROUP, SSM_STATE)),
                  spec((S5_ROWS, 1))],
        out_specs=slab,
        out_shape=jax.ShapeDtypeStruct((n_batch, seq, D_SSM), F32),
        scratch_shapes=[pltpu.VMEM((gb, S5_ROWS, n_batch * n_chunks), F32),
                        pltpu.VMEM((gb, S5_ROWS, n_batch * n_chunks), F32),
                        pltpu.VMEM((2 * S5_ROWS, S5_ROWS), F32)],
        compiler_params=_cparams(("parallel",)),
        name="s5_mixer",
    )(proj3, um, lr, li, lrc, lic, ldt, bre, bim, cre, cim, dsk)


def _gelu_tanh(x):
    c = math.sqrt(2.0 / math.pi)
    return x * (0.5 * (1.0 + jnp.tanh(c * (x + 0.044715 * (x * x * x)))))


def _glu_kernel(y_ref, wa_ref, wb_ref, o_ref, z_ref):
    @pl.when(pl.program_id(1) == 0)
    def _():
        z_ref[...] = _gelu_tanh(y_ref[...]).astype(BF16)

    z = z_ref[...]
    o_ref[...] = (_dot(z, wa_ref[...]) * jax.nn.sigmoid(_dot(z, wb_ref[...]))).astype(o_ref.dtype)


def _glu_proj(y, w_glu, tm, tn):
    t, k = y.shape
    n = w_glu.shape[1] // 2
    nb = n // tn
    return pl.pallas_call(
        _glu_kernel,
        grid=(t // tm, nb),
        in_specs=[pl.BlockSpec((tm, k), lambda i, j: (i, 0)),
                  pl.BlockSpec((k, tn), lambda i, j: (0, j)),
                  pl.BlockSpec((k, tn), lambda i, j: (0, j + nb))],
        out_specs=pl.BlockSpec((tm, tn), lambda i, j: (i, j)),
        out_shape=jax.ShapeDtypeStruct((t, n), BF16),
        scratch_shapes=[pltpu.VMEM((tm, k), BF16)],
        compiler_params=_cparams(("parallel", "arbitrary")),
        name="s5_glu_proj",
    )(y, w_glu, w_glu)


CONV_HALO = 32
CONV_ROWS = 64
CONV_SLABS = D_CONV // LANES


def _conv_kernel(c1_ref, c2_ref, m1_ref, m2_ref, w_ref, b_ref, lg_ref, lb_ref, wo_ref, o_ref,
                 cbuf_ref, conv_ref, act_ref, *, tr):
    r = pl.program_id(1)

    @pl.when(r == 0)
    def _():
        cm = m1_ref[...] * jax.nn.sigmoid(m2_ref[...])
        for j in range(CONV_SLABS):
            cbuf_ref[j, 0:CONV_HALO - N_META, :] = jnp.zeros((CONV_HALO - N_META, LANES), F32)
            cbuf_ref[j, CONV_HALO - N_META:CONV_HALO, :] = cm[:, j * LANES:(j + 1) * LANES]

    @pl.when(r > 0)
    def _():
        cbuf_ref[:, 0:CONV_HALO, :] = cbuf_ref[:, tr:tr + CONV_HALO, :]

    c = c1_ref[...] * jax.nn.sigmoid(c2_ref[...])
    for j in range(CONV_SLABS):
        cbuf_ref[j, CONV_HALO:, :] = c[:, j * LANES:(j + 1) * LANES]

    off = CONV_HALO - (CONV_WIDTH - 1)
    half = CONV_ROWS // 2

    def chunk(rc, carry):
        base = pl.multiple_of(rc * CONV_ROWS, CONV_ROWS)
        for j in range(CONV_SLABS):
            lanes = slice(j * LANES, (j + 1) * LANES)
            bias = jnp.broadcast_to(b_ref[:, lanes], (half, LANES))
            acc_e = bias
            acc_o = bias
            for k in range(CONV_WIDTH):
                wk = w_ref[k:k + 1, lanes]
                acc_e = acc_e + wk * cbuf_ref[j, pl.ds(base + off + k, half, stride=2), :]
                acc_o = acc_o + wk * cbuf_ref[j, pl.ds(base + off + k + 1, half, stride=2), :]
            conv_ref[j, pl.ds(base, half, stride=2), :] = acc_e
            conv_ref[j, pl.ds(base + 1, half, stride=2), :] = acc_o
        return carry

    lax.fori_loop(0, tr // CONV_ROWS, chunk, 0)

    s1 = jnp.zeros((tr, 1), F32)
    for j in range(CONV_SLABS):
        s1 = s1 + jnp.sum(conv_ref[j], axis=-1, keepdims=True)
    mu = s1 * (1.0 / D_CONV)
    s2 = jnp.zeros((tr, 1), F32)
    for j in range(CONV_SLABS):
        vc = conv_ref[j] - mu
        s2 = s2 + jnp.sum(vc * vc, axis=-1, keepdims=True)
    inv = lax.rsqrt(s2 * (1.0 / D_CONV) + EPS)
    for j in range(CONV_SLABS):
        lanes = slice(j * LANES, (j + 1) * LANES)
        yln = (conv_ref[j] - mu) * inv * lg_ref[:, lanes] + lb_ref[:, lanes]
        act_ref[:, lanes] = (yln * jax.nn.sigmoid(yln)).astype(BF16)
    o_ref[...] = _dot(act_ref[...], wo_ref[...]).astype(o_ref.dtype)


def _conv_branch(proj, proj_meta, conv_w, conv_b, ln_g, ln_b, w_out, n_batch, seq, tr):
    nr = seq // tr
    kern = functools.partial(_conv_kernel, tr=tr)
    vec = lambda: pl.BlockSpec((1, D_CONV), lambda b, r: (0, 0))
    return pl.pallas_call(
        kern,
        grid=(n_batch, nr),
        in_specs=[pl.BlockSpec((tr, D_CONV), lambda b, r: (b * nr + r, 1)),
                  pl.BlockSpec((tr, D_CONV), lambda b, r: (b * nr + r, 2)),
                  pl.BlockSpec((N_META, D_CONV), lambda b, r: (0, 1)),
                  pl.BlockSpec((N_META, D_CONV), lambda b, r: (0, 2)),
                  pl.BlockSpec((CONV_WIDTH, D_CONV), lambda b, r: (0, 0)),
                  vec(), vec(), vec(),
                  pl.BlockSpec((D_CONV, D_MODEL), lambda b, r: (0, 0))],
        out_specs=pl.BlockSpec((tr, D_MODEL), lambda b, r: (b * nr + r, 0)),
        out_shape=jax.ShapeDtypeStruct((n_batch * seq, D_MODEL), BF16),
        scratch_shapes=[pltpu.VMEM((CONV_SLABS, CONV_HALO + tr, LANES), F32),
                        pltpu.VMEM((CONV_SLABS, tr, LANES), F32),
                        pltpu.VMEM((tr, D_CONV), BF16)],
        compiler_params=_cparams(("parallel", "arbitrary")),
        name="conv_branch",
    )(proj, proj, proj_meta, proj_meta, conv_w, conv_b, ln_g, ln_b, w_out)


def _merge_kernel(oa_ref, ob_ref, ga_ref, gb_ref, x_ref, w_ref, o_ref, m_ref):
    @pl.when(pl.program_id(1) == 0)
    def _():
        m = (ga_ref[...].astype(F32) * oa_ref[...].astype(F32)
             + gb_ref[...].astype(F32) * ob_ref[...].astype(F32))
        m_ref[...] = m.astype(BF16)

    o_ref[...] = x_ref[...] + _dot(m_ref[...], w_ref[...])


def _merge_proj(out_a, out_b, gates, x, w_out, tm, tn):
    t, d = x.shape
    row = lambda blk: pl.BlockSpec((tm, d), lambda i, j: (i, blk))
    return pl.pallas_call(
        _merge_kernel,
        grid=(t // tm, d // tn),
        in_specs=[row(0), row(0), row(0), row(1),
                  pl.BlockSpec((tm, tn), lambda i, j: (i, j)),
                  pl.BlockSpec((d, tn), lambda i, j: (0, j))],
        out_specs=pl.BlockSpec((tm, tn), lambda i, j: (i, j)),
        out_shape=jax.ShapeDtypeStruct((t, d), F32),
        scratch_shapes=[pltpu.VMEM((tm, d), BF16)],
        compiler_params=_cparams(("parallel", "arbitrary")),
        name="merge_out_proj",
    )(out_a, out_b, gates, gates, x, w_out)


def _ffn_kernel(h_ref, g_ref, w1_ref, w2_ref, gf_ref, o_ref, xn_ref, acc_ref):
    f = pl.program_id(1)

    @pl.when(f == 0)
    def _():
        xn_ref[...] = _rms(h_ref[...], g_ref[...]).astype(BF16)
        acc_ref[...] = jnp.zeros_like(acc_ref)

    t = jnp.maximum(_dot(xn_ref[...], w1_ref[...]), 0.0)
    acc_ref[...] += _dot((t * t).astype(BF16), w2_ref[...])

    @pl.when(f == pl.num_programs(1) - 1)
    def _():
        o_ref[...] = _rms(h_ref[...] + acc_ref[...], gf_ref[...])


def _ffn(h, g, w1, w2, gf, tm, tf):
    t, d = h.shape
    ff = w1.shape[1]
    return pl.pallas_call(
        _ffn_kernel,
        grid=(t // tm, ff // tf),
        in_specs=[pl.BlockSpec((tm, d), lambda i, f: (i, 0)),
                  pl.BlockSpec((1, d), lambda i, f: (0, 0)),
                  pl.BlockSpec((d, tf), lambda i, f: (0, f)),
                  pl.BlockSpec((tf, d), lambda i, f: (f, 0)),
                  pl.BlockSpec((1, d), lambda i, f: (0, 0))],
        out_specs=pl.BlockSpec((tm, d), lambda i, f: (i, 0)),
        out_shape=jax.ShapeDtypeStruct((t, d), F32),
        scratch_shapes=[pltpu.VMEM((tm, d), BF16), pltpu.VMEM((tm, d), F32)],
        compiler_params=_cparams(("parallel", "arbitrary")),
        name="ffn_final_norm",
    )(h, g, w1, w2, gf)


def kernel(x, meta, norm_mix_g, w_in, lam_re, lam_im, log_dt, b_re, b_im, c_re, c_im, d_skip,
           w_glu, conv_w, conv_b, conv_ln_g, conv_ln_b, w_conv_out, w_out, norm_ffn_g,
           w_ff1, w_ff2, norm_f_g):
    bsz, seq, d = x.shape
    assert w_in.shape[0] == 1 and d == D_MODEL and meta.shape[0] == N_META
    xr = x.reshape(bsz * seq, d)
    g_mix = norm_mix_g[0].reshape(1, d)
    w_in_b = w_in[0].astype(BF16)

    n_plain = D_SSM + 2 * D_CONV
    proj, gates = _rms_matmul(xr, g_mix, w_in_b, 1024, 1024, n_plain)
    proj_meta, _ = _rms_matmul(meta, g_mix, w_in_b, N_META, 1024, n_plain)

    y = _s5_mixer(proj.reshape(bsz, seq, -1), proj_meta[:, :D_SSM], lam_re[0], lam_im[0],
                  log_dt[0], b_re[0], b_im[0], c_re[0], c_im[0], d_skip[0])
    y = y.reshape(bsz * seq, D_SSM)

    out_a = _glu_proj(y, w_glu[0].astype(BF16), 1024, 1024)

    out_b = _conv_branch(proj, proj_meta, conv_w[0], conv_b[0].reshape(1, -1),
                         conv_ln_g[0].reshape(1, -1), conv_ln_b[0].reshape(1, -1),
                         w_conv_out[0].astype(BF16), bsz, seq, 512)

    h1 = _merge_proj(out_a, out_b, gates, xr, w_out[0].astype(BF16), 512, 1024)

    out = _ffn(h1, norm_ffn_g[0].reshape(1, d), w_ff1[0].astype(BF16), w_ff2[0].astype(BF16),
               norm_f_g.reshape(1, d), 512, 1024)
    return out.reshape(bsz, seq, d)
```

```python
import functools
import math

import jax
import jax.numpy as jnp
from jax import lax
from jax.experimental import pallas as pl
from jax.experimental.pallas import tpu as pltpu

F32 = jnp.float32
BF16 = jnp.bfloat16

D_MODEL = 2048
N_META = 16
D_SSM = 1024
SSM_GROUP = 16
N_GROUPS = 64
SSM_STATE = 64
D_CONV = 1024
CONV_WIDTH = 31
D_FF = 8192
EPS = 1e-6
LANES = 128

S5_CHUNK = 16
S5_ROWS = S5_CHUNK * SSM_GROUP
S5_GROUP_BLOCK = LANES // SSM_GROUP

VMEM_LIMIT = 56 * 1024 * 1024


def _cparams(sem):
    return pltpu.CompilerParams(dimension_semantics=sem, vmem_limit_bytes=VMEM_LIMIT)


def _dot(a, b, precision=None):
    return jnp.dot(a, b, preferred_element_type=F32, precision=precision)


def _rms(x, g):
    ms = jnp.mean(x * x, axis=-1, keepdims=True)
    return x * lax.rsqrt(ms + EPS) * g


def _rms_matmul_kernel(x_ref, g_ref, w_ref, o_ref, s_ref, xn_ref, *, n_plain):
    j = pl.program_id(1)

    @pl.when(j == 0)
    def _():
        xn_ref[...] = _rms(x_ref[...], g_ref[...]).astype(BF16)

    acc = _dot(xn_ref[...], w_ref[...])

    @pl.when(j < n_plain)
    def _():
        o_ref[...] = acc

    @pl.when(j >= n_plain)
    def _():
        s_ref[...] = jax.nn.sigmoid(acc).astype(BF16)


def _rms_matmul(x, g, w, tm, tn, n_plain_cols):
    t, d = x.shape
    n = w.shape[1]
    n_plain = n_plain_cols // tn
    kern = functools.partial(_rms_matmul_kernel, n_plain=n_plain)
    return pl.pallas_call(
        kern,
        grid=(t // tm, n // tn),
        in_specs=[pl.BlockSpec((tm, d), lambda i, j: (i, 0)),
                  pl.BlockSpec((1, d), lambda i, j: (0, 0)),
                  pl.BlockSpec((d, tn), lambda i, j: (0, j))],
        out_specs=[pl.BlockSpec((tm, tn), lambda i, j: (i, jnp.minimum(j, n_plain - 1))),
                   pl.BlockSpec((tm, tn), lambda i, j: (i, jnp.maximum(j - n_plain, 0)))],
        out_shape=[jax.ShapeDtypeStruct((t, n_plain_cols), F32),
                   jax.ShapeDtypeStruct((t, n - n_plain_cols), BF16)],
        scratch_shapes=[pltpu.VMEM((tm, d), BF16)],
        compiler_params=_cparams(("parallel", "arbitrary")),
        name="rms_in_proj",
    )(x, g, w)


def _cmul(ar, ai, br, bi):
    return ar * br - ai * bi, ar * bi + ai * br


def _cpow_table(br, bi, k, n_bits):
    bit = (k & 1) > 0
    pr = jnp.where(bit, br, 1.0)
    pi = jnp.where(bit, bi, 0.0)
    for j in range(1, n_bits):
        br, bi = _cmul(br, bi, br, bi)
        tr, ti = _cmul(pr, pi, br, bi)
        bit = ((k >> j) & 1) > 0
        pr = jnp.where(bit, tr, pr)
        pi = jnp.where(bit, ti, pi)
    return pr, pi


def _dot3(a, b):
    ah = a.astype(BF16)
    al = (a - ah.astype(F32)).astype(BF16)
    bh = b.astype(BF16)
    bl = (b - bh.astype(F32)).astype(BF16)
    return _dot(ah, bh) + (_dot(ah, bl) + _dot(al, bh))


def _s5_kernel(u_ref, um_ref, lr_ref, li_ref, lrc_ref, lic_ref, ldt_ref, bre_ref, bim_ref,
               cre_ref, cim_ref, d_ref, wa_ref, wb_ref, wc_ref,
               y_ref, wab_ref, wbb_ref, wcb_ref, ut_ref, yt_ref, z_ref, *, n_batch, n_chunks):
    gb = S5_GROUP_BLOCK
    n_lanes = n_batch * n_chunks

    wab_ref[...] = wa_ref[...].astype(BF16)
    wbb_ref[...] = wb_ref[...].astype(BF16)
    wcb_ref[...] = wc_ref[...].astype(BF16)

    for b in range(n_batch):
        for s in range(S5_CHUNK):
            piece = u_ref[b, pl.ds(s, n_chunks, stride=S5_CHUNK), :]
            pt = piece.T
            for g in range(gb):
                ut_ref[g, s * SSM_GROUP:(s + 1) * SSM_GROUP, b * n_chunks:(b + 1) * n_chunks] = (
                    pt[g * SSM_GROUP:(g + 1) * SSM_GROUP, :])

    lane_blk = lax.broadcasted_iota(jnp.int32, (1, S5_ROWS), 1) // SSM_GROUP
    kq = S5_CHUNK - 1 - lane_blk
    tau = lax.broadcasted_iota(jnp.int32, (S5_ROWS + SSM_GROUP, 1), 0) // SSM_GROUP
    lane_c = lax.broadcasted_iota(jnp.int32, (1, n_lanes), 1) % n_chunks
    first = lane_c == 0
    z_ref[0:S5_ROWS, :] = jnp.zeros((S5_ROWS, S5_ROWS), F32)

    for gi in range(gb):
        dt = jnp.exp(ldt_ref[gi])
        lrc = lrc_ref[gi]
        lic = lic_ref[gi]
        ar = lrc * dt
        ai = lic * dt
        e = jnp.exp(ar)
        lbr = e * jnp.cos(ai)
        lbi = e * jnp.sin(ai)
        den = lrc * lrc + lic * lic
        nr = lbr - 1.0
        fr = (nr * lrc + lbi * lic) / den
        fi = (lbi * lrc - nr * lic) / den
        bbr, bbi = _cmul(fr, fi, bre_ref[gi], bim_ref[gi])
        qpr, qpi = _cpow_table(lbr, lbi, kq, 4)
        qtr, qti = _cmul(qpr, qpi, bbr, bbi)

        er = jnp.exp(lr_ref[gi] * dt)
        air = li_ref[gi] * dt
        cpr, cpi = _cpow_table(er * jnp.cos(air), er * jnp.sin(air), tau, 5)
        clr, cli = _cmul(cpr, cpi, cre_ref[gi], cim_ref[gi])

        kt = _dot3(clr[:S5_ROWS], bbr) - _dot3(cli[:S5_ROWS], bbi)
        z_ref[S5_ROWS:, :] = kt
        mt = jnp.zeros((S5_ROWS, S5_ROWS), F32)
        for s in range(S5_CHUNK):
            lo = S5_ROWS - SSM_GROUP * s
            mt = jnp.where(lane_blk == s, z_ref[lo:lo + S5_ROWS, :], mt)

        ut = ut_ref[gi]
        utb = ut.astype(BF16)
        yt_ref[gi] = _dot(mt.astype(BF16), utb) + d_ref[gi] * ut

        um = um_ref[gi]
        x0r = jnp.sum(qtr * um, axis=1, keepdims=True)
        x0i = jnp.sum(qti * um, axis=1, keepdims=True)
        pr, pi = lbr, lbi
        for _ in range(4):
            pr, pi = _cmul(pr, pi, pr, pi)
        axr, axi = _cmul(pr, pi, x0r, x0i)
        qr = _dot(qtr.astype(BF16), utb)
        qi = _dot(qti.astype(BF16), utb)
        qr = jnp.where(first, qr + axr, qr)
        qi = jnp.where(first, qi + axi, qi)
        sh = 1
        while sh < n_chunks:
            rr = pltpu.roll(qr, sh, 1)
            ri = pltpu.roll(qi, sh, 1)
            tr, ti = _cmul(pr, pi, rr, ri)
            keep = lane_c >= sh
            qr = qr + jnp.where(keep, tr, 0.0)
            qi = qi + jnp.where(keep, ti, 0.0)
            pr, pi = _cmul(pr, pi, pr, pi)
            sh *= 2
        xpr = jnp.where(first, x0r, pltpu.roll(qr, 1, 1))
        xpi = jnp.where(first, x0i, pltpu.roll(qi, 1, 1))
        yt_ref[gi] += (_dot(clr[SSM_GROUP:].astype(BF16), xpr.astype(BF16))
                       - _dot(cli[SSM_GROUP:].astype(BF16), xpi.astype(BF16)))

    for b in range(n_batch):
        for t in range(S5_CHUNK):
            piece = jnp.concatenate(
                [yt_ref[g, t * SSM_GROUP:(t + 1) * SSM_GROUP, b * n_chunks:(b + 1) * n_chunks]
                 for g in range(gb)], axis=0)
            y_ref[b, pl.ds(t, n_chunks, stride=S5_CHUNK), :] = piece.T


def _s5_mixer(proj3, u_meta, lam_re, lam_im, log_dt, b_re, b_im, c_re, c_im, d_skip, side_weights):
    n_batch, seq, _ = proj3.shape
    n_chunks = seq // S5_CHUNK
    assert n_chunks == LANES
    g = N_GROUPS
    gb = S5_GROUP_BLOCK
    um = u_meta.reshape(S5_CHUNK, g, SSM_GROUP).transpose(1, 0, 2).reshape(g, 1, S5_ROWS)
    lr = lam_re.reshape(g, 1, SSM_STATE)
    li = lam_im.reshape(g, 1, SSM_STATE)
    lrc = lam_re.reshape(g, SSM_STATE, 1)
    lic = lam_im.reshape(g, SSM_STATE, 1)
    ldt = log_dt.reshape(g, 1, 1)
    bre = jnp.tile(b_re, (1, 1, S5_CHUNK))
    bim = jnp.tile(b_im, (1, 1, S5_CHUNK))
    cre = jnp.tile(c_re, (1, S5_CHUNK + 1, 1))
    cim = jnp.tile(c_im, (1, S5_CHUNK + 1, 1))
    dsk = jnp.tile(d_skip.reshape(g, SSM_GROUP, 1), (1, S5_CHUNK, 1))

    def spec(shape):
        return pl.BlockSpec((gb,) + shape, lambda i: (i, 0, 0))

    slab = pl.BlockSpec((n_batch, seq, LANES), lambda i: (0, 0, i))
    n_steps = g // gb
    side_specs = [pl.BlockSpec((w.shape[0] // n_steps, w.shape[1]), lambda i: (i, 0))
                  for w in side_weights]
    kern = functools.partial(_s5_kernel, n_batch=n_batch, n_chunks=n_chunks)
    return pl.pallas_call(
        kern,
        grid=(n_steps,),
        in_specs=[slab,
                  spec((1, S5_ROWS)),
                  spec((1, SSM_STATE)), spec((1, SSM_STATE)),
                  spec((SSM_STATE, 1)), spec((SSM_STATE, 1)),
                  spec((1, 1)),
                  spec((SSM_STATE, S5_ROWS)), spec((SSM_STATE, S5_ROWS)),
                  spec((S5_ROWS + SSM_GROUP, SSM_STATE)),
                  spec((S5_ROWS + SSM_GROUP, SSM_STATE)),
                  spec((S5_ROWS, 1))] + side_specs,
        out_specs=[slab] + side_specs,
        out_shape=[jax.ShapeDtypeStruct((n_batch, seq, D_SSM), F32)]
        + [jax.ShapeDtypeStruct(w.shape, BF16) for w in side_weights],
        scratch_shapes=[pltpu.VMEM((gb, S5_ROWS, n_batch * n_chunks), F32),
                        pltpu.VMEM((gb, S5_ROWS, n_batch * n_chunks), F32),
                        pltpu.VMEM((2 * S5_ROWS, S5_ROWS), F32)],
        compiler_params=_cparams(("parallel",)),
        name="s5_mixer",
    )(proj3, um, lr, li, lrc, lic, ldt, bre, bim, cre, cim, dsk, *side_weights)


def _gelu_tanh(x):
    c = math.sqrt(2.0 / math.pi)
    return x * (0.5 * (1.0 + jnp.tanh(c * (x + 0.044715 * (x * x * x)))))


def _glu_kernel(y_ref, wa_ref, wb_ref, o_ref, z_ref):
    @pl.when(pl.program_id(1) == 0)
    def _():
        z_ref[...] = _gelu_tanh(y_ref[...]).astype(BF16)

    z = z_ref[...]
    o_ref[...] = (_dot(z, wa_ref[...]) * jax.nn.sigmoid(_dot(z, wb_ref[...]))).astype(o_ref.dtype)


def _glu_proj(y, w_glu, tm, tn):
    t, k = y.shape
    n = w_glu.shape[1] // 2
    nb = n // tn
    return pl.pallas_call(
        _glu_kernel,
        grid=(t // tm, nb),
        in_specs=[pl.BlockSpec((tm, k), lambda i, j: (i, 0)),
                  pl.BlockSpec((k, tn), lambda i, j: (0, j)),
                  pl.BlockSpec((k, tn), lambda i, j: (0, j + nb))],
        out_specs=pl.BlockSpec((tm, tn), lambda i, j: (i, j)),
        out_shape=jax.ShapeDtypeStruct((t, n), BF16),
        scratch_shapes=[pltpu.VMEM((tm, k), BF16)],
        compiler_params=_cparams(("parallel", "arbitrary")),
        name="s5_glu_proj",
    )(y, w_glu, w_glu)


CONV_HALO = 32
CONV_ROWS = 64
CONV_SLABS = D_CONV // LANES


def _conv_kernel(c1_ref, c2_ref, m1_ref, m2_ref, w_ref, b_ref, lg_ref, lb_ref, wo_ref,
                 w1_ref, w2_ref, o_ref, w1b_ref, w2b_ref, cbuf_ref, conv_ref, act_ref, *, tr):
    r = pl.program_id(1)

    w1b_ref[...] = w1_ref[...].astype(BF16)
    w2b_ref[...] = w2_ref[...].astype(BF16)

    @pl.when(r == 0)
    def _():
        cm = m1_ref[...] * jax.nn.sigmoid(m2_ref[...])
        for j in range(CONV_SLABS):
            cbuf_ref[j, 0:CONV_HALO - N_META, :] = jnp.zeros((CONV_HALO - N_META, LANES), F32)
            cbuf_ref[j, CONV_HALO - N_META:CONV_HALO, :] = cm[:, j * LANES:(j + 1) * LANES]

    @pl.when(r > 0)
    def _():
        cbuf_ref[:, 0:CONV_HALO, :] = cbuf_ref[:, tr:tr + CONV_HALO, :]

    c = c1_ref[...] * jax.nn.sigmoid(c2_ref[...])
    for j in range(CONV_SLABS):
        cbuf_ref[j, CONV_HALO:, :] = c[:, j * LANES:(j + 1) * LANES]

    off = CONV_HALO - (CONV_WIDTH - 1)
    half = CONV_ROWS // 2

    def chunk(rc, carry):
        base = pl.multiple_of(rc * CONV_ROWS, CONV_ROWS)
        for j in range(CONV_SLABS):
            lanes = slice(j * LANES, (j + 1) * LANES)
            bias = jnp.broadcast_to(b_ref[:, lanes], (half, LANES))
            acc_e = bias
            acc_o = bias
            for k in range(CONV_WIDTH):
                wk = w_ref[k:k + 1, lanes]
                acc_e = acc_e + wk * cbuf_ref[j, pl.ds(base + off + k, half, stride=2), :]
                acc_o = acc_o + wk * cbuf_ref[j, pl.ds(base + off + k + 1, half, stride=2), :]
            conv_ref[j, pl.ds(base, half, stride=2), :] = acc_e
            conv_ref[j, pl.ds(base + 1, half, stride=2), :] = acc_o
        return carry

    lax.fori_loop(0, tr // CONV_ROWS, chunk, 0)

    s1 = jnp.zeros((tr, 1), F32)
    for j in range(CONV_SLABS):
        s1 = s1 + jnp.sum(conv_ref[j], axis=-1, keepdims=True)
    mu = s1 * (1.0 / D_CONV)
    s2 = jnp.zeros((tr, 1), F32)
    for j in range(CONV_SLABS):
        vc = conv_ref[j] - mu
        s2 = s2 + jnp.sum(vc * vc, axis=-1, keepdims=True)
    inv = lax.rsqrt(s2 * (1.0 / D_CONV) + EPS)
    for j in range(CONV_SLABS):
        lanes = slice(j * LANES, (j + 1) * LANES)
        yln = (conv_ref[j] - mu) * inv * lg_ref[:, lanes] + lb_ref[:, lanes]
        act_ref[:, lanes] = (yln * jax.nn.sigmoid(yln)).astype(BF16)
    o_ref[...] = _dot(act_ref[...], wo_ref[...]).astype(o_ref.dtype)


def _conv_branch(proj, proj_meta, conv_w, conv_b, ln_g, ln_b, w_out, w_ff1, w_ff2, n_batch, seq, tr):
    nr = seq // tr
    n_steps = n_batch * nr
    kern = functools.partial(_conv_kernel, tr=tr)
    vec = lambda: pl.BlockSpec((1, D_CONV), lambda b, r: (0, 0))
    side = lambda w: pl.BlockSpec((w.shape[0] // n_steps, w.shape[1]), lambda b, r: (b * nr + r, 0))
    return pl.pallas_call(
        kern,
        grid=(n_batch, nr),
        in_specs=[pl.BlockSpec((tr, D_CONV), lambda b, r: (b * nr + r, 1)),
                  pl.BlockSpec((tr, D_CONV), lambda b, r: (b * nr + r, 2)),
                  pl.BlockSpec((N_META, D_CONV), lambda b, r: (0, 1)),
                  pl.BlockSpec((N_META, D_CONV), lambda b, r: (0, 2)),
                  pl.BlockSpec((CONV_WIDTH, D_CONV), lambda b, r: (0, 0)),
                  vec(), vec(), vec(),
                  pl.BlockSpec((D_CONV, D_MODEL), lambda b, r: (0, 0)),
                  side(w_ff1), side(w_ff2)],
        out_specs=[pl.BlockSpec((tr, D_MODEL), lambda b, r: (b * nr + r, 0)),
                   side(w_ff1), side(w_ff2)],
        out_shape=[jax.ShapeDtypeStruct((n_batch * seq, D_MODEL), BF16),
                   jax.ShapeDtypeStruct(w_ff1.shape, BF16),
                   jax.ShapeDtypeStruct(w_ff2.shape, BF16)],
        scratch_shapes=[pltpu.VMEM((CONV_SLABS, CONV_HALO + tr, LANES), F32),
                        pltpu.VMEM((CONV_SLABS, tr, LANES), F32),
                        pltpu.VMEM((tr, D_CONV), BF16)],
        compiler_params=_cparams(("parallel", "arbitrary")),
        name="conv_branch",
    )(proj, proj, proj_meta, proj_meta, conv_w, conv_b, ln_g, ln_b, w_out, w_ff1, w_ff2)


MERGE_K_CHUNK = 512


def _merge_kernel(oa_ref, ob_ref, ga_ref, gb_ref, x_ref, w_ref, o_ref):
    acc = x_ref[...]
    for k in range(D_MODEL // MERGE_K_CHUNK):
        cols = slice(k * MERGE_K_CHUNK, (k + 1) * MERGE_K_CHUNK)
        m = (ga_ref[:, cols].astype(F32) * oa_ref[:, cols].astype(F32)
             + gb_ref[:, cols].astype(F32) * ob_ref[:, cols].astype(F32))
        acc = acc + _dot(m.astype(BF16), w_ref[cols, :])
    o_ref[...] = acc


def _merge_proj(out_a, out_b, gates, x, w_out, tm):
    t, d = x.shape
    row = lambda blk: pl.BlockSpec((tm, d), lambda i: (i, blk))
    return pl.pallas_call(
        _merge_kernel,
        grid=(t // tm,),
        in_specs=[row(0), row(0), row(0), row(1), row(0),
                  pl.BlockSpec((d, d), lambda i: (0, 0))],
        out_specs=row(0),
        out_shape=jax.ShapeDtypeStruct((t, d), F32),
        compiler_params=_cparams(("parallel",)),
        name="merge_out_proj",
    )(out_a, out_b, gates, gates, x, w_out)


def _ffn_kernel(h_ref, g_ref, w1_ref, w2_ref, gf_ref, o_ref, xn_ref, acc_ref):
    f = pl.program_id(1)

    @pl.when(f == 0)
    def _():
        xn_ref[...] = _rms(h_ref[...], g_ref[...]).astype(BF16)
        acc_ref[...] = jnp.zeros_like(acc_ref)

    t = jnp.maximum(_dot(xn_ref[...], w1_ref[...]), 0.0)
    acc_ref[...] += _dot((t * t).astype(BF16), w2_ref[...])

    @pl.when(f == pl.num_programs(1) - 1)
    def _():
        o_ref[...] = _rms(h_ref[...] + acc_ref[...], gf_ref[...])


def _ffn(h, g, w1, w2, gf, tm, tf):
    t, d = h.shape
    ff = w1.shape[1]
    return pl.pallas_call(
        _ffn_kernel,
        grid=(t // tm, ff // tf),
        in_specs=[pl.BlockSpec((tm, d), lambda i, f: (i, 0)),
                  pl.BlockSpec((1, d), lambda i, f: (0, 0)),
                  pl.BlockSpec((d, tf), lambda i, f: (0, f)),
                  pl.BlockSpec((tf, d), lambda i, f: (f, 0)),
                  pl.BlockSpec((1, d), lambda i, f: (0, 0))],
        out_specs=pl.BlockSpec((tm, d), lambda i, f: (i, 0)),
        out_shape=jax.ShapeDtypeStruct((t, d), F32),
        scratch_shapes=[pltpu.VMEM((tm, d), BF16), pltpu.VMEM((tm, d), F32)],
        compiler_params=_cparams(("parallel", "arbitrary")),
        name="ffn_final_norm",
    )(h, g, w1, w2, gf)


def kernel(x, meta, norm_mix_g, w_in, lam_re, lam_im, log_dt, b_re, b_im, c_re, c_im, d_skip,
           w_glu, conv_w, conv_b, conv_ln_g, conv_ln_b, w_conv_out, w_out, norm_ffn_g,
           w_ff1, w_ff2, norm_f_g):
    bsz, seq, d = x.shape
    assert w_in.shape[0] == 1 and d == D_MODEL and meta.shape[0] == N_META
    xr = x.reshape(bsz * seq, d)
    g_mix = norm_mix_g[0].reshape(1, d)
    w_in_b = w_in[0].astype(BF16)

    n_plain = D_SSM + 2 * D_CONV
    proj, gates = _rms_matmul(xr, g_mix, w_in_b, 1024, 1024, n_plain)
    proj_meta, _ = _rms_matmul(meta, g_mix, w_in_b, N_META, 1024, n_plain)

    y, w_glu_b, w_out_b, w_co_b = _s5_mixer(
        proj.reshape(bsz, seq, -1), proj_meta[:, :D_SSM], lam_re[0], lam_im[0], log_dt[0],
        b_re[0], b_im[0], c_re[0], c_im[0], d_skip[0], (w_glu[0], w_out[0], w_conv_out[0]))
    y = y.reshape(bsz * seq, D_SSM)

    out_a = _glu_proj(y, w_glu_b, 1024, 1024)

    out_b, w_ff1_b, w_ff2_b = _conv_branch(
        proj, proj_meta, conv_w[0], conv_b[0].reshape(1, -1), conv_ln_g[0].reshape(1, -1),
        conv_ln_b[0].reshape(1, -1), w_co_b, w_ff1[0], w_ff2[0], bsz, seq, 512)

    h1 = _merge_proj(out_a, out_b, gates, xr, w_out_b, 512)

    out = _ffn(h1, norm_ffn_g[0].reshape(1, d), w_ff1_b, w_ff2_b, norm_f_g.reshape(1, d), 512, 1024)
    return out.reshape(bsz, seq, d)
```

```python
import functools
import math

import jax
import jax.numpy as jnp
from jax import lax
from jax.experimental import pallas as pl
from jax.experimental.pallas import tpu as pltpu

F32 = jnp.float32
BF16 = jnp.bfloat16

D_MODEL = 2048
N_META = 16
D_SSM = 1024
SSM_GROUP = 16
N_GROUPS = 64
SSM_STATE = 64
D_CONV = 1024
CONV_WIDTH = 31
D_FF = 8192
EPS = 1e-6
LANES = 128

S5_CHUNK = 16
S5_ROWS = S5_CHUNK * SSM_GROUP
S5_GROUP_BLOCK = LANES // SSM_GROUP

VMEM_LIMIT = 56 * 1024 * 1024


def _cparams(sem):
    return pltpu.CompilerParams(dimension_semantics=sem, vmem_limit_bytes=VMEM_LIMIT)


def _dot(a, b, precision=None):
    return jnp.dot(a, b, preferred_element_type=F32, precision=precision)


def _sigmoid(x):
    return 0.5 * jnp.tanh(0.5 * x) + 0.5


def _rms(x, g):
    ms = jnp.mean(x * x, axis=-1, keepdims=True)
    return x * lax.rsqrt(ms + EPS) * g


def _rms_matmul_kernel(x_ref, g_ref, w_ref, o_ref, s_ref, xn_ref, *, n_plain):
    j = pl.program_id(1)

    @pl.when(j == 0)
    def _():
        xn_ref[...] = _rms(x_ref[...], g_ref[...]).astype(BF16)

    acc = _dot(xn_ref[...], w_ref[...])

    @pl.when(j < n_plain)
    def _():
        o_ref[...] = acc

    @pl.when(j >= n_plain)
    def _():
        s_ref[...] = _sigmoid(acc).astype(BF16)


def _rms_matmul(x, g, w, tm, tn, n_plain_cols):
    t, d = x.shape
    n = w.shape[1]
    n_plain = n_plain_cols // tn
    kern = functools.partial(_rms_matmul_kernel, n_plain=n_plain)
    return pl.pallas_call(
        kern,
        grid=(t // tm, n // tn),
        in_specs=[pl.BlockSpec((tm, d), lambda i, j: (i, 0)),
                  pl.BlockSpec((1, d), lambda i, j: (0, 0)),
                  pl.BlockSpec((d, tn), lambda i, j: (0, j))],
        out_specs=[pl.BlockSpec((tm, tn), lambda i, j: (i, jnp.minimum(j, n_plain - 1))),
                   pl.BlockSpec((tm, tn), lambda i, j: (i, jnp.maximum(j - n_plain, 0)))],
        out_shape=[jax.ShapeDtypeStruct((t, n_plain_cols), F32),
                   jax.ShapeDtypeStruct((t, n - n_plain_cols), BF16)],
        scratch_shapes=[pltpu.VMEM((tm, d), BF16)],
        compiler_params=_cparams(("parallel", "arbitrary")),
        name="rms_in_proj",
    )(x, g, w)


def _meta_proj_kernel(x_ref, g_ref, w_ref, o_ref, wb_ref, xn_ref):
    @pl.when(pl.program_id(0) == 0)
    def _():
        xn_ref[...] = _rms(x_ref[...], g_ref[...]).astype(BF16)

    wb = w_ref[...].astype(BF16)
    wb_ref[...] = wb
    o_ref[...] = _dot(xn_ref[...], wb)


def _meta_proj(meta, g, w, tn):
    t, d = meta.shape
    n = w.shape[1]
    return pl.pallas_call(
        _meta_proj_kernel,
        grid=(n // tn,),
        in_specs=[pl.BlockSpec((t, d), lambda j: (0, 0)),
                  pl.BlockSpec((1, d), lambda j: (0, 0)),
                  pl.BlockSpec((d, tn), lambda j: (0, j))],
        out_specs=[pl.BlockSpec((t, tn), lambda j: (0, j)),
                   pl.BlockSpec((d, tn), lambda j: (0, j))],
        out_shape=[jax.ShapeDtypeStruct((t, n), F32), jax.ShapeDtypeStruct((d, n), BF16)],
        scratch_shapes=[pltpu.VMEM((t, d), BF16)],
        compiler_params=_cparams(("arbitrary",)),
        name="meta_in_proj",
    )(meta, g, w)


def _cmul(ar, ai, br, bi):
    return ar * br - ai * bi, ar * bi + ai * br


def _cpow_table(br, bi, k, n_bits):
    bit = (k & 1) > 0
    pr = jnp.where(bit, br, 1.0)
    pi = jnp.where(bit, bi, 0.0)
    for j in range(1, n_bits):
        br, bi = _cmul(br, bi, br, bi)
        tr, ti = _cmul(pr, pi, br, bi)
        bit = ((k >> j) & 1) > 0
        pr = jnp.where(bit, tr, pr)
        pi = jnp.where(bit, ti, pi)
    return pr, pi


def _dot3(a, b):
    ah = a.astype(BF16)
    al = (a - ah.astype(F32)).astype(BF16)
    bh = b.astype(BF16)
    bl = (b - bh.astype(F32)).astype(BF16)
    return _dot(ah, bh) + (_dot(ah, bl) + _dot(al, bh))


def _s5_kernel(u_ref, um_ref, lr_ref, li_ref, lrc_ref, lic_ref, ldt_ref, bre_ref, bim_ref,
               cre_ref, cim_ref, d_ref, wa_ref, wb_ref, wc_ref,
               y_ref, wab_ref, wbb_ref, wcb_ref, ut_ref, yt_ref, z_ref, *, n_batch, n_chunks):
    gb = S5_GROUP_BLOCK
    n_lanes = n_batch * n_chunks

    wab_ref[...] = wa_ref[...].astype(BF16)
    wbb_ref[...] = wb_ref[...].astype(BF16)
    wcb_ref[...] = wc_ref[...].astype(BF16)

    for b in range(n_batch):
        for s in range(S5_CHUNK):
            piece = u_ref[b, pl.ds(s, n_chunks, stride=S5_CHUNK), :]
            pt = piece.T
            for g in range(gb):
                ut_ref[g, s * SSM_GROUP:(s + 1) * SSM_GROUP, b * n_chunks:(b + 1) * n_chunks] = (
                    pt[g * SSM_GROUP:(g + 1) * SSM_GROUP, :])

    lane_blk = lax.broadcasted_iota(jnp.int32, (1, S5_ROWS), 1) // SSM_GROUP
    kq = S5_CHUNK - 1 - lane_blk
    tau = lax.broadcasted_iota(jnp.int32, (S5_ROWS + SSM_GROUP, 1), 0) // SSM_GROUP
    lane_c = lax.broadcasted_iota(jnp.int32, (1, n_lanes), 1) % n_chunks
    first = lane_c == 0
    z_ref[0:S5_ROWS, :] = jnp.zeros((S5_ROWS, S5_ROWS), F32)

    for gi in range(gb):
        dt = jnp.exp(ldt_ref[gi])
        lrc = lrc_ref[gi]
        lic = lic_ref[gi]
        ar = lrc * dt
        ai = lic * dt
        e = jnp.exp(ar)
        lbr = e * jnp.cos(ai)
        lbi = e * jnp.sin(ai)
        den = lrc * lrc + lic * lic
        nr = lbr - 1.0
        fr = (nr * lrc + lbi * lic) / den
        fi = (lbi * lrc - nr * lic) / den
        bbr, bbi = _cmul(fr, fi, bre_ref[gi], bim_ref[gi])
        qpr, qpi = _cpow_table(lbr, lbi, kq, 4)
        qtr, qti = _cmul(qpr, qpi, bbr, bbi)

        er = jnp.exp(lr_ref[gi] * dt)
        air = li_ref[gi] * dt
        cpr, cpi = _cpow_table(er * jnp.cos(air), er * jnp.sin(air), tau, 5)
        clr, cli = _cmul(cpr, cpi, cre_ref[gi], cim_ref[gi])

        kt = _dot3(clr[:S5_ROWS], bbr) - _dot3(cli[:S5_ROWS], bbi)
        z_ref[S5_ROWS:, :] = kt
        mt = jnp.zeros((S5_ROWS, S5_ROWS), F32)
        for s in range(S5_CHUNK):
            lo = S5_ROWS - SSM_GROUP * s
            mt = jnp.where(lane_blk == s, z_ref[lo:lo + S5_ROWS, :], mt)

        ut = ut_ref[gi]
        utb = ut.astype(BF16)
        yt_ref[gi] = _dot(mt.astype(BF16), utb) + d_ref[gi] * ut

        um = um_ref[gi]
        x0r = jnp.sum(qtr * um, axis=1, keepdims=True)
        x0i = jnp.sum(qti * um, axis=1, keepdims=True)
        pr, pi = lbr, lbi
        for _ in range(4):
            pr, pi = _cmul(pr, pi, pr, pi)
        axr, axi = _cmul(pr, pi, x0r, x0i)
        qr = _dot(qtr.astype(BF16), utb)
        qi = _dot(qti.astype(BF16), utb)
        qr = jnp.where(first, qr + axr, qr)
        qi = jnp.where(first, qi + axi, qi)
        sh = 1
        while sh < n_chunks:
            rr = pltpu.roll(qr, sh, 1)
            ri = pltpu.roll(qi, sh, 1)
            tr, ti = _cmul(pr, pi, rr, ri)
            keep = lane_c >= sh
            qr = qr + jnp.where(keep, tr, 0.0)
            qi = qi + jnp.where(keep, ti, 0.0)
            pr, pi = _cmul(pr, pi, pr, pi)
            sh *= 2
        xpr = jnp.where(first, x0r, pltpu.roll(qr, 1, 1))
        xpi = jnp.where(first, x0i, pltpu.roll(qi, 1, 1))
        yt_ref[gi] += (_dot(clr[SSM_GROUP:].astype(BF16), xpr.astype(BF16))
                       - _dot(cli[SSM_GROUP:].astype(BF16), xpi.astype(BF16)))

    for b in range(n_batch):
        for t in range(S5_CHUNK):
            piece = jnp.concatenate(
                [yt_ref[g, t * SSM_GROUP:(t + 1) * SSM_GROUP, b * n_chunks:(b + 1) * n_chunks]
                 for g in range(gb)], axis=0)
            y_ref[b, pl.ds(t, n_chunks, stride=S5_CHUNK), :] = piece.T


def _s5_mixer(proj3, u_meta, lam_re, lam_im, log_dt, b_re, b_im, c_re, c_im, d_skip, side_weights):
    n_batch, seq, _ = proj3.shape
    n_chunks = seq // S5_CHUNK
    assert n_chunks == LANES
    g = N_GROUPS
    gb = S5_GROUP_BLOCK
    um = u_meta.reshape(S5_CHUNK, g, SSM_GROUP).transpose(1, 0, 2).reshape(g, 1, S5_ROWS)
    lr = lam_re.reshape(g, 1, SSM_STATE)
    li = lam_im.reshape(g, 1, SSM_STATE)
    lrc = lam_re.reshape(g, SSM_STATE, 1)
    lic = lam_im.reshape(g, SSM_STATE, 1)
    ldt = log_dt.reshape(g, 1, 1)
    bre = jnp.tile(b_re, (1, 1, S5_CHUNK))
    bim = jnp.tile(b_im, (1, 1, S5_CHUNK))
    cre = jnp.tile(c_re, (1, S5_CHUNK + 1, 1))
    cim = jnp.tile(c_im, (1, S5_CHUNK + 1, 1))
    dsk = jnp.tile(d_skip.reshape(g, SSM_GROUP, 1), (1, S5_CHUNK, 1))

    def spec(shape):
        return pl.BlockSpec((gb,) + shape, lambda i: (i, 0, 0))

    slab = pl.BlockSpec((n_batch, seq, LANES), lambda i: (0, 0, i))
    n_steps = g // gb
    side_specs = [pl.BlockSpec((w.shape[0] // n_steps, w.shape[1]), lambda i: (i, 0))
                  for w in side_weights]
    kern = functools.partial(_s5_kernel, n_batch=n_batch, n_chunks=n_chunks)
    return pl.pallas_call(
        kern,
        grid=(n_steps,),
        in_specs=[slab,
                  spec((1, S5_ROWS)),
                  spec((1, SSM_STATE)), spec((1, SSM_STATE)),
                  spec((SSM_STATE, 1)), spec((SSM_STATE, 1)),
                  spec((1, 1)),
                  spec((SSM_STATE, S5_ROWS)), spec((SSM_STATE, S5_ROWS)),
                  spec((S5_ROWS + SSM_GROUP, SSM_STATE)),
                  spec((S5_ROWS + SSM_GROUP, SSM_STATE)),
                  spec((S5_ROWS, 1))] + side_specs,
        out_specs=[slab] + side_specs,
        out_shape=[jax.ShapeDtypeStruct((n_batch, seq, D_SSM), F32)]
        + [jax.ShapeDtypeStruct(w.shape, BF16) for w in side_weights],
        scratch_shapes=[pltpu.VMEM((gb, S5_ROWS, n_batch * n_chunks), F32),
                        pltpu.VMEM((gb, S5_ROWS, n_batch * n_chunks), F32),
                        pltpu.VMEM((2 * S5_ROWS, S5_ROWS), F32)],
        compiler_params=_cparams(("parallel",)),
        name="s5_mixer",
    )(proj3, um, lr, li, lrc, lic, ldt, bre, bim, cre, cim, dsk, *side_weights)


def _gelu_tanh(x):
    c = math.sqrt(2.0 / math.pi)
    return x * (0.5 * (1.0 + jnp.tanh(c * (x + 0.044715 * (x * x * x)))))


def _glu_kernel(y_ref, wa_ref, wb_ref, o_ref, z_ref):
    @pl.when(pl.program_id(1) == 0)
    def _():
        z_ref[...] = _gelu_tanh(y_ref[...]).astype(BF16)

    z = z_ref[...]
    o_ref[...] = (_dot(z, wa_ref[...]) * _sigmoid(_dot(z, wb_ref[...]))).astype(o_ref.dtype)


def _glu_proj(y, w_glu, tm, tn):
    t, k = y.shape
    n = w_glu.shape[1] // 2
    nb = n // tn
    return pl.pallas_call(
        _glu_kernel,
        grid=(t // tm, nb),
        in_specs=[pl.BlockSpec((tm, k), lambda i, j: (i, 0)),
                  pl.BlockSpec((k, tn), lambda i, j: (0, j)),
                  pl.BlockSpec((k, tn), lambda i, j: (0, j + nb))],
        out_specs=pl.BlockSpec((tm, tn), lambda i, j: (i, j)),
        out_shape=jax.ShapeDtypeStruct((t, n), BF16),
        scratch_shapes=[pltpu.VMEM((tm, k), BF16)],
        compiler_params=_cparams(("parallel", "arbitrary")),
        name="s5_glu_proj",
    )(y, w_glu, w_glu)


CONV_HALO = 32
CONV_ROWS = 64
CONV_SLABS = D_CONV // LANES


def _conv_kernel(c1_ref, c2_ref, m1_ref, m2_ref, w_ref, b_ref, lg_ref, lb_ref, wo_ref,
                 w1_ref, w2_ref, o_ref, w1b_ref, w2b_ref, cbuf_ref, conv_ref, act_ref, *, tr):
    r = pl.program_id(1)

    w1b_ref[...] = w1_ref[...].astype(BF16)
    w2b_ref[...] = w2_ref[...].astype(BF16)

    @pl.when(r == 0)
    def _():
        cm = m1_ref[...] * _sigmoid(m2_ref[...])
        for j in range(CONV_SLABS):
            cbuf_ref[j, 0:CONV_HALO - N_META, :] = jnp.zeros((CONV_HALO - N_META, LANES), F32)
            cbuf_ref[j, CONV_HALO - N_META:CONV_HALO, :] = cm[:, j * LANES:(j + 1) * LANES]

    @pl.when(r > 0)
    def _():
        cbuf_ref[:, 0:CONV_HALO, :] = cbuf_ref[:, tr:tr + CONV_HALO, :]

    c = c1_ref[...] * _sigmoid(c2_ref[...])
    for j in range(CONV_SLABS):
        cbuf_ref[j, CONV_HALO:, :] = c[:, j * LANES:(j + 1) * LANES]

    off = CONV_HALO - (CONV_WIDTH - 1)
    half = CONV_ROWS // 2

    def chunk(rc, carry):
        base = pl.multiple_of(rc * CONV_ROWS, CONV_ROWS)
        for j in range(CONV_SLABS):
            lanes = slice(j * LANES, (j + 1) * LANES)
            bias = jnp.broadcast_to(b_ref[:, lanes], (half, LANES))
            acc_e = bias
            acc_o = bias
            for k in range(CONV_WIDTH):
                wk = w_ref[k:k + 1, lanes]
                acc_e = acc_e + wk * cbuf_ref[j, pl.ds(base + off + k, half, stride=2), :]
                acc_o = acc_o + wk * cbuf_ref[j, pl.ds(base + off + k + 1, half, stride=2), :]
            conv_ref[j, pl.ds(base, half, stride=2), :] = acc_e
            conv_ref[j, pl.ds(base + 1, half, stride=2), :] = acc_o
        return carry

    lax.fori_loop(0, tr // CONV_ROWS, chunk, 0)

    s1 = jnp.zeros((tr, 1), F32)
    for j in range(CONV_SLABS):
        s1 = s1 + jnp.sum(conv_ref[j], axis=-1, keepdims=True)
    mu = s1 * (1.0 / D_CONV)
    s2 = jnp.zeros((tr, 1), F32)
    for j in range(CONV_SLABS):
        vc = conv_ref[j] - mu
        s2 = s2 + jnp.sum(vc * vc, axis=-1, keepdims=True)
    inv = lax.rsqrt(s2 * (1.0 / D_CONV) + EPS)
    for j in range(CONV_SLABS):
        lanes = slice(j * LANES, (j + 1) * LANES)
        yln = (conv_ref[j] - mu) * inv * lg_ref[:, lanes] + lb_ref[:, lanes]
        act_ref[:, lanes] = (yln * _sigmoid(yln)).astype(BF16)
    o_ref[...] = _dot(act_ref[...], wo_ref[...]).astype(o_ref.dtype)


def _conv_branch(proj, proj_meta, conv_w, conv_b, ln_g, ln_b, w_out, w_ff1, w_ff2, n_batch, seq, tr):
    nr = seq // tr
    n_steps = n_batch * nr
    kern = functools.partial(_conv_kernel, tr=tr)
    vec = lambda: pl.BlockSpec((1, D_CONV), lambda b, r: (0, 0))
    side = lambda w: pl.BlockSpec((w.shape[0] // n_steps, w.shape[1]), lambda b, r: (b * nr + r, 0))
    return pl.pallas_call(
        kern,
        grid=(n_batch, nr),
        in_specs=[pl.BlockSpec((tr, D_CONV), lambda b, r: (b * nr + r, 1)),
                  pl.BlockSpec((tr, D_CONV), lambda b, r: (b * nr + r, 2)),
                  pl.BlockSpec((N_META, D_CONV), lambda b, r: (0, 1)),
                  pl.BlockSpec((N_META, D_CONV), lambda b, r: (0, 2)),
                  pl.BlockSpec((CONV_WIDTH, D_CONV), lambda b, r: (0, 0)),
                  vec(), vec(), vec(),
                  pl.BlockSpec((D_CONV, D_MODEL), lambda b, r: (0, 0)),
                  side(w_ff1), side(w_ff2)],
        out_specs=[pl.BlockSpec((tr, D_MODEL), lambda b, r: (b * nr + r, 0)),
                   side(w_ff1), side(w_ff2)],
        out_shape=[jax.ShapeDtypeStruct((n_batch * seq, D_MODEL), BF16),
                   jax.ShapeDtypeStruct(w_ff1.shape, BF16),
                   jax.ShapeDtypeStruct(w_ff2.shape, BF16)],
        scratch_shapes=[pltpu.VMEM((CONV_SLABS, CONV_HALO + tr, LANES), F32),
                        pltpu.VMEM((CONV_SLABS, tr, LANES), F32),
                        pltpu.VMEM((tr, D_CONV), BF16)],
        compiler_params=_cparams(("parallel", "arbitrary")),
        name="conv_branch",
    )(proj, proj, proj_meta, proj_meta, conv_w, conv_b, ln_g, ln_b, w_out, w_ff1, w_ff2)


MERGE_K_CHUNK = 512


def _merge_kernel(oa_ref, ob_ref, ga_ref, gb_ref, x_ref, w_ref, o_ref):
    acc = x_ref[...]
    for k in range(D_MODEL // MERGE_K_CHUNK):
        cols = slice(k * MERGE_K_CHUNK, (k + 1) * MERGE_K_CHUNK)
        m = (ga_ref[:, cols].astype(F32) * oa_ref[:, cols].astype(F32)
             + gb_ref[:, cols].astype(F32) * ob_ref[:, cols].astype(F32))
        acc = acc + _dot(m.astype(BF16), w_ref[cols, :])
    o_ref[...] = acc


def _merge_proj(out_a, out_b, gates, x, w_out, tm):
    t, d = x.shape
    row = lambda blk: pl.BlockSpec((tm, d), lambda i: (i, blk))
    return pl.pallas_call(
        _merge_kernel,
        grid=(t // tm,),
        in_specs=[row(0), row(0), row(0), row(1), row(0),
                  pl.BlockSpec((d, d), lambda i: (0, 0))],
        out_specs=row(0),
        out_shape=jax.ShapeDtypeStruct((t, d), F32),
        compiler_params=_cparams(("parallel",)),
        name="merge_out_proj",
    )(out_a, out_b, gates, gates, x, w_out)


def _ffn_kernel(h_ref, g_ref, w1_ref, w2_ref, gf_ref, o_ref, xn_ref, acc_ref):
    f = pl.program_id(1)

    @pl.when(f == 0)
    def _():
        xn_ref[...] = _rms(h_ref[...], g_ref[...]).astype(BF16)
        acc_ref[...] = jnp.zeros_like(acc_ref)

    t = jnp.maximum(_dot(xn_ref[...], w1_ref[...]), 0.0)
    acc_ref[...] += _dot((t * t).astype(BF16), w2_ref[...])

    @pl.when(f == pl.num_programs(1) - 1)
    def _():
        o_ref[...] = _rms(h_ref[...] + acc_ref[...], gf_ref[...])


def _ffn(h, g, w1, w2, gf, tm, tf):
    t, d = h.shape
    ff = w1.shape[1]
    return pl.pallas_call(
        _ffn_kernel,
        grid=(t // tm, ff // tf),
        in_specs=[pl.BlockSpec((tm, d), lambda i, f: (i, 0)),
                  pl.BlockSpec((1, d), lambda i, f: (0, 0)),
                  pl.BlockSpec((d, tf), lambda i, f: (0, f)),
                  pl.BlockSpec((tf, d), lambda i, f: (f, 0)),
                  pl.BlockSpec((1, d), lambda i, f: (0, 0))],
        out_specs=pl.BlockSpec((tm, d), lambda i, f: (i, 0)),
        out_shape=jax.ShapeDtypeStruct((t, d), F32),
        scratch_shapes=[pltpu.VMEM((tm, d), BF16), pltpu.VMEM((tm, d), F32)],
        compiler_params=_cparams(("parallel", "arbitrary")),
        name="ffn_final_norm",
    )(h, g, w1, w2, gf)


def kernel(x, meta, norm_mix_g, w_in, lam_re, lam_im, log_dt, b_re, b_im, c_re, c_im, d_skip,
           w_glu, conv_w, conv_b, conv_ln_g, conv_ln_b, w_conv_out, w_out, norm_ffn_g,
           w_ff1, w_ff2, norm_f_g):
    bsz, seq, d = x.shape
    assert w_in.shape[0] == 1 and d == D_MODEL and meta.shape[0] == N_META
    xr = x.reshape(bsz * seq, d)
    g_mix = norm_mix_g[0].reshape(1, d)

    proj_meta, w_in_b = _meta_proj(meta, g_mix, w_in[0], 1024)
    n_plain = D_SSM + 2 * D_CONV
    proj, gates = _rms_matmul(xr, g_mix, w_in_b, 1024, 1024, n_plain)

    y, w_glu_b, w_out_b, w_co_b = _s5_mixer(
        proj.reshape(bsz, seq, -1), proj_meta[:, :D_SSM], lam_re[0], lam_im[0], log_dt[0],
        b_re[0], b_im[0], c_re[0], c_im[0], d_skip[0], (w_glu[0], w_out[0], w_conv_out[0]))
    y = y.reshape(bsz * seq, D_SSM)

    out_a = _glu_proj(y, w_glu_b, 1024, 1024)

    out_b, w_ff1_b, w_ff2_b = _conv_branch(
        proj, proj_meta, conv_w[0], conv_b[0].reshape(1, -1), conv_ln_g[0].reshape(1, -1),
        conv_ln_b[0].reshape(1, -1), w_co_b, w_ff1[0], w_ff2[0], bsz, seq, 512)

    h1 = _merge_proj(out_a, out_b, gates, xr, w_out_b, 512)

    out = _ffn(h1, norm_ffn_g[0].reshape(1, d), w_ff1_b, w_ff2_b, norm_f_g.reshape(1, d), 512, 1024)
    return out.reshape(bsz, seq, d)
```

```python
import functools
import math

import jax
import jax.numpy as jnp
from jax import lax
from jax.experimental import pallas as pl
from jax.experimental.pallas import tpu as pltpu

F32 = jnp.float32
BF16 = jnp.bfloat16

D_MODEL = 2048
N_META = 16
D_SSM = 1024
SSM_GROUP = 16
N_GROUPS = 64
SSM_STATE = 64
D_CONV = 1024
CONV_WIDTH = 31
D_FF = 8192
EPS = 1e-6
LANES = 128

S5_CHUNK = 16
S5_ROWS = S5_CHUNK * SSM_GROUP
S5_GROUP_BLOCK = LANES // SSM_GROUP

VMEM_LIMIT = 56 * 1024 * 1024


def _cparams(sem):
    return pltpu.CompilerParams(dimension_semantics=sem, vmem_limit_bytes=VMEM_LIMIT)


def _dot(a, b, precision=None):
    return jnp.dot(a, b, preferred_element_type=F32, precision=precision)


def _sigmoid(x):
    return 0.5 * jnp.tanh(0.5 * x) + 0.5


def _rms(x, g):
    ms = jnp.mean(x * x, axis=-1, keepdims=True)
    return x * lax.rsqrt(ms + EPS) * g


def _rms_matmul_kernel(x_ref, g_ref, w_ref, o_ref, s_ref, xn_ref, *, n_plain):
    j = pl.program_id(1)

    @pl.when(j == 0)
    def _():
        xn_ref[...] = _rms(x_ref[...], g_ref[...]).astype(BF16)

    acc = _dot(xn_ref[...], w_ref[...])

    @pl.when(j < n_plain)
    def _():
        o_ref[...] = acc

    @pl.when(j >= n_plain)
    def _():
        s_ref[...] = _sigmoid(acc).astype(BF16)


def _rms_matmul(x, g, w, tm, tn, n_plain_cols):
    t, d = x.shape
    n = w.shape[1]
    n_plain = n_plain_cols // tn
    kern = functools.partial(_rms_matmul_kernel, n_plain=n_plain)
    return pl.pallas_call(
        kern,
        grid=(t // tm, n // tn),
        in_specs=[pl.BlockSpec((tm, d), lambda i, j: (i, 0)),
                  pl.BlockSpec((1, d), lambda i, j: (0, 0)),
                  pl.BlockSpec((d, tn), lambda i, j: (0, j))],
        out_specs=[pl.BlockSpec((tm, tn), lambda i, j: (i, jnp.minimum(j, n_plain - 1))),
                   pl.BlockSpec((tm, tn), lambda i, j: (i, jnp.maximum(j - n_plain, 0)))],
        out_shape=[jax.ShapeDtypeStruct((t, n_plain_cols), F32),
                   jax.ShapeDtypeStruct((t, n - n_plain_cols), BF16)],
        scratch_shapes=[pltpu.VMEM((tm, d), BF16)],
        compiler_params=_cparams(("parallel", "arbitrary")),
        name="rms_in_proj",
    )(x, g, w)


def _meta_proj_kernel(x_ref, g_ref, w_ref, o_ref, wb_ref, xn_ref):
    @pl.when(pl.program_id(0) == 0)
    def _():
        xn_ref[...] = _rms(x_ref[...], g_ref[...]).astype(BF16)

    wb = w_ref[...].astype(BF16)
    wb_ref[...] = wb
    o_ref[...] = _dot(xn_ref[...], wb)


def _meta_proj(meta, g, w, tn):
    t, d = meta.shape
    n = w.shape[1]
    return pl.pallas_call(
        _meta_proj_kernel,
        grid=(n // tn,),
        in_specs=[pl.BlockSpec((t, d), lambda j: (0, 0)),
                  pl.BlockSpec((1, d), lambda j: (0, 0)),
                  pl.BlockSpec((d, tn), lambda j: (0, j))],
        out_specs=[pl.BlockSpec((t, tn), lambda j: (0, j)),
                   pl.BlockSpec((d, tn), lambda j: (0, j))],
        out_shape=[jax.ShapeDtypeStruct((t, n), F32), jax.ShapeDtypeStruct((d, n), BF16)],
        scratch_shapes=[pltpu.VMEM((t, d), BF16)],
        compiler_params=_cparams(("arbitrary",)),
        name="meta_in_proj",
    )(meta, g, w)


def _cmul(ar, ai, br, bi):
    return ar * br - ai * bi, ar * bi + ai * br


def _cpow_table(br, bi, k, n_bits):
    bit = (k & 1) > 0
    pr = jnp.where(bit, br, 1.0)
    pi = jnp.where(bit, bi, 0.0)
    for j in range(1, n_bits):
        br, bi = _cmul(br, bi, br, bi)
        tr, ti = _cmul(pr, pi, br, bi)
        bit = ((k >> j) & 1) > 0
        pr = jnp.where(bit, tr, pr)
        pi = jnp.where(bit, ti, pi)
    return pr, pi


def _dot3(a, b):
    ah = a.astype(BF16)
    al = (a - ah.astype(F32)).astype(BF16)
    bh = b.astype(BF16)
    bl = (b - bh.astype(F32)).astype(BF16)
    return _dot(ah, bh) + (_dot(ah, bl) + _dot(al, bh))


def _s5_kernel(u_ref, um_ref, lr_ref, li_ref, lrc_ref, lic_ref, ldt_ref, bre_ref, bim_ref,
               cre_ref, cim_ref, d_ref, wa_ref, wb_ref, wc_ref,
               y_ref, wab_ref, wbb_ref, wcb_ref, ut_ref, yt_ref, z_ref, clr_ref, cli_ref,
               *, n_batch, n_chunks):
    gb = S5_GROUP_BLOCK
    n_lanes = n_batch * n_chunks

    wab_ref[...] = wa_ref[...].astype(BF16)
    wbb_ref[...] = wb_ref[...].astype(BF16)
    wcb_ref[...] = wc_ref[...].astype(BF16)

    for b in range(n_batch):
        for s in range(S5_CHUNK):
            piece = u_ref[b, pl.ds(s, n_chunks, stride=S5_CHUNK), :]
            pt = piece.T
            for g in range(gb):
                ut_ref[g, s * SSM_GROUP:(s + 1) * SSM_GROUP, b * n_chunks:(b + 1) * n_chunks] = (
                    pt[g * SSM_GROUP:(g + 1) * SSM_GROUP, :])

    lane_blk = lax.broadcasted_iota(jnp.int32, (1, S5_ROWS), 1) // SSM_GROUP
    kq = S5_CHUNK - 1 - lane_blk
    lane_c = lax.broadcasted_iota(jnp.int32, (1, n_lanes), 1) % n_chunks
    first = lane_c == 0
    z_ref[0:S5_ROWS, :] = jnp.zeros((S5_ROWS, S5_ROWS), F32)

    for gi in range(gb):
        dt = jnp.exp(ldt_ref[gi])
        lrc = lrc_ref[gi]
        lic = lic_ref[gi]
        ar = lrc * dt
        ai = lic * dt
        e = jnp.exp(ar)
        lbr = e * jnp.cos(ai)
        lbi = e * jnp.sin(ai)
        den = lrc * lrc + lic * lic
        nr = lbr - 1.0
        fr = (nr * lrc + lbi * lic) / den
        fi = (lbi * lrc - nr * lic) / den
        bbr, bbi = _cmul(fr, fi, bre_ref[gi], bim_ref[gi])
        qpr, qpi = _cpow_table(lbr, lbi, kq, 4)
        qtr, qti = _cmul(qpr, qpi, bbr, bbi)

        er = jnp.exp(lr_ref[gi] * dt)
        air = li_ref[gi] * dt
        lrr = er * jnp.cos(air)
        lri = er * jnp.sin(air)
        cr = cre_ref[gi]
        ci = cim_ref[gi]
        for tau in range(S5_CHUNK + 1):
            clr_ref[tau * SSM_GROUP:(tau + 1) * SSM_GROUP, :] = cr
            cli_ref[tau * SSM_GROUP:(tau + 1) * SSM_GROUP, :] = ci
            cr, ci = _cmul(cr, ci, lrr, lri)
        clr = clr_ref[...]
        cli = cli_ref[...]

        kt = _dot3(clr[:S5_ROWS], bbr) - _dot3(cli[:S5_ROWS], bbi)
        z_ref[S5_ROWS:, :] = kt
        mt = jnp.zeros((S5_ROWS, S5_ROWS), F32)
        for s in range(S5_CHUNK):
            lo = S5_ROWS - SSM_GROUP * s
            mt = jnp.where(lane_blk == s, z_ref[lo:lo + S5_ROWS, :], mt)

        ut = ut_ref[gi]
        utb = ut.astype(BF16)
        yt_ref[gi] = _dot(mt.astype(BF16), utb)

        um = um_ref[gi]
        x0r = jnp.sum(qtr * um, axis=1, keepdims=True)
        x0i = jnp.sum(qti * um, axis=1, keepdims=True)
        pr, pi = lbr, lbi
        for _ in range(4):
            pr, pi = _cmul(pr, pi, pr, pi)
        axr, axi = _cmul(pr, pi, x0r, x0i)
        qr = _dot(qtr.astype(BF16), utb)
        qi = _dot(qti.astype(BF16), utb)
        qr = jnp.where(first, qr + axr, qr)
        qi = jnp.where(first, qi + axi, qi)
        sh = 1
        while sh < n_chunks:
            rr = pltpu.roll(qr, sh, 1)
            ri = pltpu.roll(qi, sh, 1)
            tr, ti = _cmul(pr, pi, rr, ri)
            keep = lane_c >= sh
            qr = qr + jnp.where(keep, tr, 0.0)
            qi = qi + jnp.where(keep, ti, 0.0)
            pr, pi = _cmul(pr, pi, pr, pi)
            sh *= 2
        xpr = jnp.where(first, x0r, pltpu.roll(qr, 1, 1))
        xpi = jnp.where(first, x0i, pltpu.roll(qi, 1, 1))
        yt_ref[gi] += (_dot(clr[SSM_GROUP:].astype(BF16), xpr.astype(BF16))
                       - _dot(cli[SSM_GROUP:].astype(BF16), xpi.astype(BF16)))

    d_row = d_ref[0]
    for b in range(n_batch):
        for t in range(S5_CHUNK):
            piece = jnp.concatenate(
                [yt_ref[g, t * SSM_GROUP:(t + 1) * SSM_GROUP, b * n_chunks:(b + 1) * n_chunks]
                 for g in range(gb)], axis=0)
            rows = pl.ds(t, n_chunks, stride=S5_CHUNK)
            y_ref[b, rows, :] = piece.T + d_row * u_ref[b, rows, :]


def _s5_mixer(proj3, u_meta, lam_re, lam_im, log_dt, b_re, b_im, c_re, c_im, d_skip, side_weights):
    n_batch, seq, _ = proj3.shape
    n_chunks = seq // S5_CHUNK
    assert n_chunks == LANES
    g = N_GROUPS
    gb = S5_GROUP_BLOCK
    um = u_meta.reshape(S5_CHUNK, g, SSM_GROUP).transpose(1, 0, 2).reshape(g, 1, S5_ROWS)
    lr = lam_re.reshape(g, 1, SSM_STATE)
    li = lam_im.reshape(g, 1, SSM_STATE)
    lrc = lam_re.reshape(g, SSM_STATE, 1)
    lic = lam_im.reshape(g, SSM_STATE, 1)
    ldt = log_dt.reshape(g, 1, 1)
    bre = jnp.tile(b_re, (1, 1, S5_CHUNK))
    bim = jnp.tile(b_im, (1, 1, S5_CHUNK))
    dsk = d_skip.reshape(g // gb, 1, LANES)

    def spec(shape):
        return pl.BlockSpec((gb,) + shape, lambda i: (i, 0, 0))

    slab = pl.BlockSpec((n_batch, seq, LANES), lambda i: (0, 0, i))
    n_steps = g // gb
    side_specs = [pl.BlockSpec((w.shape[0] // n_steps, w.shape[1]), lambda i: (i, 0))
                  for w in side_weights]
    kern = functools.partial(_s5_kernel, n_batch=n_batch, n_chunks=n_chunks)
    return pl.pallas_call(
        kern,
        grid=(n_steps,),
        in_specs=[slab,
                  spec((1, S5_ROWS)),
                  spec((1, SSM_STATE)), spec((1, SSM_STATE)),
                  spec((SSM_STATE, 1)), spec((SSM_STATE, 1)),
                  spec((1, 1)),
                  spec((SSM_STATE, S5_ROWS)), spec((SSM_STATE, S5_ROWS)),
                  spec((SSM_GROUP, SSM_STATE)), spec((SSM_GROUP, SSM_STATE)),
                  pl.BlockSpec((1, 1, LANES), lambda i: (i, 0, 0))] + side_specs,
        out_specs=[slab] + side_specs,
        out_shape=[jax.ShapeDtypeStruct((n_batch, seq, D_SSM), F32)]
        + [jax.ShapeDtypeStruct(w.shape, BF16) for w in side_weights],
        scratch_shapes=[pltpu.VMEM((gb, S5_ROWS, n_batch * n_chunks), F32),
                        pltpu.VMEM((gb, S5_ROWS, n_batch * n_chunks), F32),
                        pltpu.VMEM((2 * S5_ROWS, S5_ROWS), F32),
                        pltpu.VMEM((S5_ROWS + SSM_GROUP, SSM_STATE), F32),
                        pltpu.VMEM((S5_ROWS + SSM_GROUP, SSM_STATE), F32)],
        compiler_params=_cparams(("parallel",)),
        name="s5_mixer",
    )(proj3, um, lr, li, lrc, lic, ldt, bre, bim, c_re, c_im, dsk, *side_weights)


def _gelu_tanh(x):
    c = math.sqrt(2.0 / math.pi)
    return x * (0.5 * (1.0 + jnp.tanh(c * (x + 0.044715 * (x * x * x)))))


def _glu_kernel(y_ref, wa_ref, wb_ref, o_ref, z_ref):
    @pl.when(pl.program_id(1) == 0)
    def _():
        z_ref[...] = _gelu_tanh(y_ref[...]).astype(BF16)

    z = z_ref[...]
    o_ref[...] = (_dot(z, wa_ref[...]) * _sigmoid(_dot(z, wb_ref[...]))).astype(o_ref.dtype)


def _glu_proj(y, w_glu, tm, tn):
    t, k = y.shape
    n = w_glu.shape[1] // 2
    nb = n // tn
    return pl.pallas_call(
        _glu_kernel,
        grid=(t // tm, nb),
        in_specs=[pl.BlockSpec((tm, k), lambda i, j: (i, 0)),
                  pl.BlockSpec((k, tn), lambda i, j: (0, j)),
                  pl.BlockSpec((k, tn), lambda i, j: (0, j + nb))],
        out_specs=pl.BlockSpec((tm, tn), lambda i, j: (i, j)),
        out_shape=jax.ShapeDtypeStruct((t, n), BF16),
        scratch_shapes=[pltpu.VMEM((tm, k), BF16)],
        compiler_params=_cparams(("parallel", "arbitrary")),
        name="s5_glu_proj",
    )(y, w_glu, w_glu)


CONV_HALO = 32
CONV_ROWS = 64
CONV_SLABS = D_CONV // LANES


def _conv_kernel(c1_ref, c2_ref, m1_ref, m2_ref, w_ref, b_ref, lg_ref, lb_ref, wo_ref,
                 w1_ref, w2_ref, o_ref, w1b_ref, w2b_ref, cbuf_ref, conv_ref, act_ref, *, tr):
    r = pl.program_id(1)

    w1b_ref[...] = w1_ref[...].astype(BF16)
    w2b_ref[...] = w2_ref[...].astype(BF16)

    @pl.when(r == 0)
    def _():
        cm = m1_ref[...] * _sigmoid(m2_ref[...])
        for j in range(CONV_SLABS):
            cbuf_ref[j, 0:CONV_HALO - N_META, :] = jnp.zeros((CONV_HALO - N_META, LANES), F32)
            cbuf_ref[j, CONV_HALO - N_META:CONV_HALO, :] = cm[:, j * LANES:(j + 1) * LANES]

    @pl.when(r > 0)
    def _():
        cbuf_ref[:, 0:CONV_HALO, :] = cbuf_ref[:, tr:tr + CONV_HALO, :]

    c = c1_ref[...] * _sigmoid(c2_ref[...])
    for j in range(CONV_SLABS):
        cbuf_ref[j, CONV_HALO:, :] = c[:, j * LANES:(j + 1) * LANES]

    off = CONV_HALO - (CONV_WIDTH - 1)
    half = CONV_ROWS // 2

    def chunk(rc, carry):
        base = pl.multiple_of(rc * CONV_ROWS, CONV_ROWS)
        for j in range(CONV_SLABS):
            lanes = slice(j * LANES, (j + 1) * LANES)
            bias = jnp.broadcast_to(b_ref[:, lanes], (half, LANES))
            acc_e = bias
            acc_o = bias
            for k in range(CONV_WIDTH):
                wk = w_ref[k:k + 1, lanes]
                acc_e = acc_e + wk * cbuf_ref[j, pl.ds(base + off + k, half, stride=2), :]
                acc_o = acc_o + wk * cbuf_ref[j, pl.ds(base + off + k + 1, half, stride=2), :]
            conv_ref[j, pl.ds(base, half, stride=2), :] = acc_e
            conv_ref[j, pl.ds(base + 1, half, stride=2), :] = acc_o
        return carry

    lax.fori_loop(0, tr // CONV_ROWS, chunk, 0)

    s1 = jnp.zeros((tr, 1), F32)
    for j in range(CONV_SLABS):
        s1 = s1 + jnp.sum(conv_ref[j], axis=-1, keepdims=True)
    mu = s1 * (1.0 / D_CONV)
    s2 = jnp.zeros((tr, 1), F32)
    for j in range(CONV_SLABS):
        vc = conv_ref[j] - mu
        s2 = s2 + jnp.sum(vc * vc, axis=-1, keepdims=True)
    inv = lax.rsqrt(s2 * (1.0 / D_CONV) + EPS)
    for j in range(CONV_SLABS):
        lanes = slice(j * LANES, (j + 1) * LANES)
        yln = (conv_ref[j] - mu) * inv * lg_ref[:, lanes] + lb_ref[:, lanes]
        act_ref[:, lanes] = (yln * _sigmoid(yln)).astype(BF16)
    o_ref[...] = _dot(act_ref[...], wo_ref[...]).astype(o_ref.dtype)


def _conv_branch(proj, proj_meta, conv_w, conv_b, ln_g, ln_b, w_out, w_ff1, w_ff2, n_batch, seq, tr):
    nr = seq // tr
    n_steps = n_batch * nr
    kern = functools.partial(_conv_kernel, tr=tr)
    vec = lambda: pl.BlockSpec((1, D_CONV), lambda b, r: (0, 0))
    side = lambda w: pl.BlockSpec((w.shape[0] // n_steps, w.shape[1]), lambda b, r: (b * nr + r, 0))
    return pl.pallas_call(
        kern,
        grid=(n_batch, nr),
        in_specs=[pl.BlockSpec((tr, D_CONV), lambda b, r: (b * nr + r, 1)),
                  pl.BlockSpec((tr, D_CONV), lambda b, r: (b * nr + r, 2)),
                  pl.BlockSpec((N_META, D_CONV), lambda b, r: (0, 1)),
                  pl.BlockSpec((N_META, D_CONV), lambda b, r: (0, 2)),
                  pl.BlockSpec((CONV_WIDTH, D_CONV), lambda b, r: (0, 0)),
                  vec(), vec(), vec(),
                  pl.BlockSpec((D_CONV, D_MODEL), lambda b, r: (0, 0)),
                  side(w_ff1), side(w_ff2)],
        out_specs=[pl.BlockSpec((tr, D_MODEL), lambda b, r: (b * nr + r, 0)),
                   side(w_ff1), side(w_ff2)],
        out_shape=[jax.ShapeDtypeStruct((n_batch * seq, D_MODEL), BF16),
                   jax.ShapeDtypeStruct(w_ff1.shape, BF16),
                   jax.ShapeDtypeStruct(w_ff2.shape, BF16)],
        scratch_shapes=[pltpu.VMEM((CONV_SLABS, CONV_HALO + tr, LANES), F32),
                        pltpu.VMEM((CONV_SLABS, tr, LANES), F32),
                        pltpu.VMEM((tr, D_CONV), BF16)],
        compiler_params=_cparams(("parallel", "arbitrary")),
        name="conv_branch",
    )(proj, proj, proj_meta, proj_meta, conv_w, conv_b, ln_g, ln_b, w_out, w_ff1, w_ff2)


MERGE_K_CHUNK = 512


def _merge_kernel(oa_ref, ob_ref, ga_ref, gb_ref, x_ref, w_ref, o_ref):
    acc = x_ref[...]
    for k in range(D_MODEL // MERGE_K_CHUNK):
        cols = slice(k * MERGE_K_CHUNK, (k + 1) * MERGE_K_CHUNK)
        m = (ga_ref[:, cols].astype(F32) * oa_ref[:, cols].astype(F32)
             + gb_ref[:, cols].astype(F32) * ob_ref[:, cols].astype(F32))
        acc = acc + _dot(m.astype(BF16), w_ref[cols, :])
    o_ref[...] = acc


def _merge_proj(out_a, out_b, gates, x, w_out, tm):
    t, d = x.shape
    row = lambda blk: pl.BlockSpec((tm, d), lambda i: (i, blk))
    return pl.pallas_call(
        _merge_kernel,
        grid=(t // tm,),
        in_specs=[row(0), row(0), row(0), row(1), row(0),
                  pl.BlockSpec((d, d), lambda i: (0, 0))],
        out_specs=row(0),
        out_shape=jax.ShapeDtypeStruct((t, d), F32),
        compiler_params=_cparams(("parallel",)),
        name="merge_out_proj",
    )(out_a, out_b, gates, gates, x, w_out)


def _ffn_kernel(h_ref, g_ref, w1_ref, w2_ref, gf_ref, o_ref, xn_ref, acc_ref):
    f = pl.program_id(1)

    @pl.when(f == 0)
    def _():
        xn_ref[...] = _rms(h_ref[...], g_ref[...]).astype(BF16)
        acc_ref[...] = jnp.zeros_like(acc_ref)

    t = jnp.maximum(_dot(xn_ref[...], w1_ref[...]), 0.0)
    acc_ref[...] += _dot((t * t).astype(BF16), w2_ref[...])

    @pl.when(f == pl.num_programs(1) - 1)
    def _():
        o_ref[...] = _rms(h_ref[...] + acc_ref[...], gf_ref[...])


def _ffn(h, g, w1, w2, gf, tm, tf):
    t, d = h.shape
    ff = w1.shape[1]
    return pl.pallas_call(
        _ffn_kernel,
        grid=(t // tm, ff // tf),
        in_specs=[pl.BlockSpec((tm, d), lambda i, f: (i, 0)),
                  pl.BlockSpec((1, d), lambda i, f: (0, 0)),
                  pl.BlockSpec((d, tf), lambda i, f: (0, f)),
                  pl.BlockSpec((tf, d), lambda i, f: (f, 0)),
                  pl.BlockSpec((1, d), lambda i, f: (0, 0))],
        out_specs=pl.BlockSpec((tm, d), lambda i, f: (i, 0)),
        out_shape=jax.ShapeDtypeStruct((t, d), F32),
        scratch_shapes=[pltpu.VMEM((tm, d), BF16), pltpu.VMEM((tm, d), F32)],
        compiler_params=_cparams(("parallel", "arbitrary")),
        name="ffn_final_norm",
    )(h, g, w1, w2, gf)


def kernel(x, meta, norm_mix_g, w_in, lam_re, lam_im, log_dt, b_re, b_im, c_re, c_im, d_skip,
           w_glu, conv_w, conv_b, conv_ln_g, conv_ln_b, w_conv_out, w_out, norm_ffn_g,
           w_ff1, w_ff2, norm_f_g):
    bsz, seq, d = x.shape
    assert w_in.shape[0] == 1 and d == D_MODEL and meta.shape[0] == N_META
    xr = x.reshape(bsz * seq, d)
    g_mix = norm_mix_g[0].reshape(1, d)

    proj_meta, w_in_b = _meta_proj(meta, g_mix, w_in[0], 1024)
    n_plain = D_SSM + 2 * D_CONV
    proj, gates = _rms_matmul(xr, g_mix, w_in_b, 1024, 1024, n_plain)

    y, w_glu_b, w_out_b, w_co_b = _s5_mixer(
        proj.reshape(bsz, seq, -1), proj_meta[:, :D_SSM], lam_re[0], lam_im[0], log_dt[0],
        b_re[0], b_im[0], c_re[0], c_im[0], d_skip[0], (w_glu[0], w_out[0], w_conv_out[0]))
    y = y.reshape(bsz * seq, D_SSM)

    out_a = _glu_proj(y, w_glu_b, 1024, 1024)

    out_b, w_ff1_b, w_ff2_b = _conv_branch(
        proj, proj_meta, conv_w[0], conv_b[0].reshape(1, -1), conv_ln_g[0].reshape(1, -1),
        conv_ln_b[0].reshape(1, -1), w_co_b, w_ff1[0], w_ff2[0], bsz, seq, 512)

    h1 = _merge_proj(out_a, out_b, gates, xr, w_out_b, 512)

    out = _ffn(h1, norm_ffn_g[0].reshape(1, d), w_ff1_b, w_ff2_b, norm_f_g.reshape(1, d), 512, 1024)
    return out.reshape(bsz, seq, d)
```

```python
import functools
import math

import jax
import jax.numpy as jnp
from jax import lax
from jax.experimental import pallas as pl
from jax.experimental.pallas import tpu as pltpu

F32 = jnp.float32
BF16 = jnp.bfloat16

D_MODEL = 2048
N_META = 16
D_SSM = 1024
SSM_GROUP = 16
N_GROUPS = 64
SSM_STATE = 64
D_CONV = 1024
CONV_WIDTH = 31
D_FF = 8192
EPS = 1e-6
LANES = 128

S5_CHUNK = 16
S5_ROWS = S5_CHUNK * SSM_GROUP
S5_GROUP_BLOCK = LANES // SSM_GROUP

VMEM_LIMIT = 56 * 1024 * 1024


def _cparams(sem):
    return pltpu.CompilerParams(dimension_semantics=sem, vmem_limit_bytes=VMEM_LIMIT)


def _dot(a, b, precision=None):
    return jnp.dot(a, b, preferred_element_type=F32, precision=precision)


def _sigmoid(x):
    return 0.5 * jnp.tanh(0.5 * x) + 0.5


def _rms(x, g):
    ms = jnp.mean(x * x, axis=-1, keepdims=True)
    return x * lax.rsqrt(ms + EPS) * g


IN_PROJ_N_CHUNK = 256


def _rms_matmul_kernel(x_ref, g_ref, w_ref, o_ref, s_ref, xn_ref, *, n_plain):
    j = pl.program_id(1)

    @pl.when(j == 0)
    def _():
        xn_ref[...] = _rms(x_ref[...], g_ref[...]).astype(BF16)

    @pl.when(j < n_plain)
    def _():
        o_ref[...] = _dot(xn_ref[...], w_ref[...])

    @pl.when(j >= n_plain)
    def _():
        for c in range(w_ref.shape[1] // IN_PROJ_N_CHUNK):
            cols = slice(c * IN_PROJ_N_CHUNK, (c + 1) * IN_PROJ_N_CHUNK)
            s_ref[:, cols] = _sigmoid(_dot(xn_ref[...], w_ref[:, cols])).astype(BF16)


def _rms_matmul(x, g, w, tm, tn, n_plain_cols):
    t, d = x.shape
    n = w.shape[1]
    n_plain = n_plain_cols // tn
    kern = functools.partial(_rms_matmul_kernel, n_plain=n_plain)
    return pl.pallas_call(
        kern,
        grid=(t // tm, n // tn),
        in_specs=[pl.BlockSpec((tm, d), lambda i, j: (i, 0)),
                  pl.BlockSpec((1, d), lambda i, j: (0, 0)),
                  pl.BlockSpec((d, tn), lambda i, j: (0, j))],
        out_specs=[pl.BlockSpec((tm, tn), lambda i, j: (i, jnp.minimum(j, n_plain - 1))),
                   pl.BlockSpec((tm, tn), lambda i, j: (i, jnp.maximum(j - n_plain, 0)))],
        out_shape=[jax.ShapeDtypeStruct((t, n_plain_cols), F32),
                   jax.ShapeDtypeStruct((t, n - n_plain_cols), BF16)],
        scratch_shapes=[pltpu.VMEM((tm, d), BF16)],
        compiler_params=_cparams(("parallel", "arbitrary")),
        name="rms_in_proj",
    )(x, g, w)


def _meta_proj_kernel(x_ref, g_ref, w_ref, o_ref, wb_ref, xn_ref):
    @pl.when(pl.program_id(0) == 0)
    def _():
        xn_ref[...] = _rms(x_ref[...], g_ref[...]).astype(BF16)

    wb = w_ref[...].astype(BF16)
    wb_ref[...] = wb
    o_ref[...] = _dot(xn_ref[...], wb)


def _meta_proj(meta, g, w, tn):
    t, d = meta.shape
    n = w.shape[1]
    return pl.pallas_call(
        _meta_proj_kernel,
        grid=(n // tn,),
        in_specs=[pl.BlockSpec((t, d), lambda j: (0, 0)),
                  pl.BlockSpec((1, d), lambda j: (0, 0)),
                  pl.BlockSpec((d, tn), lambda j: (0, j))],
        out_specs=[pl.BlockSpec((t, tn), lambda j: (0, j)),
                   pl.BlockSpec((d, tn), lambda j: (0, j))],
        out_shape=[jax.ShapeDtypeStruct((t, n), F32), jax.ShapeDtypeStruct((d, n), BF16)],
        scratch_shapes=[pltpu.VMEM((t, d), BF16)],
        compiler_params=_cparams(("arbitrary",)),
        name="meta_in_proj",
    )(meta, g, w)


def _cmul(ar, ai, br, bi):
    return ar * br - ai * bi, ar * bi + ai * br


def _cpow_table(br, bi, k, n_bits):
    bit = (k & 1) > 0
    pr = jnp.where(bit, br, 1.0)
    pi = jnp.where(bit, bi, 0.0)
    for j in range(1, n_bits):
        br, bi = _cmul(br, bi, br, bi)
        tr, ti = _cmul(pr, pi, br, bi)
        bit = ((k >> j) & 1) > 0
        pr = jnp.where(bit, tr, pr)
        pi = jnp.where(bit, ti, pi)
    return pr, pi


def _dot3(a, b):
    ah = a.astype(BF16)
    al = (a - ah.astype(F32)).astype(BF16)
    bh = b.astype(BF16)
    bl = (b - bh.astype(F32)).astype(BF16)
    return _dot(ah, bh) + (_dot(ah, bl) + _dot(al, bh))


def _s5_kernel(u_ref, um_ref, lr_ref, li_ref, lrc_ref, lic_ref, ldt_ref, bre_ref, bim_ref,
               cre_ref, cim_ref, d_ref, wa_ref, wb_ref, wc_ref,
               y_ref, wab_ref, wbb_ref, wcb_ref, ut_ref, yt_ref, z_ref, clr_ref, cli_ref,
               *, n_batch, n_chunks):
    gb = S5_GROUP_BLOCK
    n_lanes = n_batch * n_chunks

    wab_ref[...] = wa_ref[...].astype(BF16)
    wbb_ref[...] = wb_ref[...].astype(BF16)
    wcb_ref[...] = wc_ref[...].astype(BF16)

    for b in range(n_batch):
        for s in range(S5_CHUNK):
            piece = u_ref[b, pl.ds(s, n_chunks, stride=S5_CHUNK), :]
            pt = piece.T
            for g in range(gb):
                ut_ref[g, s * SSM_GROUP:(s + 1) * SSM_GROUP, b * n_chunks:(b + 1) * n_chunks] = (
                    pt[g * SSM_GROUP:(g + 1) * SSM_GROUP, :])

    lane_blk = lax.broadcasted_iota(jnp.int32, (1, S5_ROWS), 1) // SSM_GROUP
    kq = S5_CHUNK - 1 - lane_blk
    lane_c = lax.broadcasted_iota(jnp.int32, (1, n_lanes), 1) % n_chunks
    first = lane_c == 0
    z_ref[0:S5_ROWS, :] = jnp.zeros((S5_ROWS, S5_ROWS), F32)

    for gi in range(gb):
        dt = jnp.exp(ldt_ref[gi])
        lrc = lrc_ref[gi]
        lic = lic_ref[gi]
        ar = lrc * dt
        ai = lic * dt
        e = jnp.exp(ar)
        lbr = e * jnp.cos(ai)
        lbi = e * jnp.sin(ai)
        den = lrc * lrc + lic * lic
        nr = lbr - 1.0
        fr = (nr * lrc + lbi * lic) / den
        fi = (lbi * lrc - nr * lic) / den
        bbr, bbi = _cmul(fr, fi, bre_ref[gi], bim_ref[gi])
        qpr, qpi = _cpow_table(lbr, lbi, kq, 4)
        qtr, qti = _cmul(qpr, qpi, bbr, bbi)

        er = jnp.exp(lr_ref[gi] * dt)
        air = li_ref[gi] * dt
        lrr = er * jnp.cos(air)
        lri = er * jnp.sin(air)
        cr = cre_ref[gi]
        ci = cim_ref[gi]
        for tau in range(S5_CHUNK + 1):
            clr_ref[tau * SSM_GROUP:(tau + 1) * SSM_GROUP, :] = cr
            cli_ref[tau * SSM_GROUP:(tau + 1) * SSM_GROUP, :] = ci
            cr, ci = _cmul(cr, ci, lrr, lri)
        clr = clr_ref[...]
        cli = cli_ref[...]

        kt = _dot3(clr[:S5_ROWS], bbr) - _dot3(cli[:S5_ROWS], bbi)
        z_ref[S5_ROWS:, :] = kt
        mt = jnp.zeros((S5_ROWS, S5_ROWS), F32)
        for s in range(S5_CHUNK):
            lo = S5_ROWS - SSM_GROUP * s
            mt = jnp.where(lane_blk == s, z_ref[lo:lo + S5_ROWS, :], mt)

        ut = ut_ref[gi]
        utb = ut.astype(BF16)
        yt_ref[gi] = _dot(mt.astype(BF16), utb)

        um = um_ref[gi]
        x0r = jnp.sum(qtr * um, axis=1, keepdims=True)
        x0i = jnp.sum(qti * um, axis=1, keepdims=True)
        pr, pi = lbr, lbi
        for _ in range(4):
            pr, pi = _cmul(pr, pi, pr, pi)
        axr, axi = _cmul(pr, pi, x0r, x0i)
        qr = _dot(qtr.astype(BF16), utb)
        qi = _dot(qti.astype(BF16), utb)
        qr = jnp.where(first, qr + axr, qr)
        qi = jnp.where(first, qi + axi, qi)
        sh = 1
        while sh < n_chunks:
            rr = pltpu.roll(qr, sh, 1)
            ri = pltpu.roll(qi, sh, 1)
            tr, ti = _cmul(pr, pi, rr, ri)
            keep = lane_c >= sh
            qr = qr + jnp.where(keep, tr, 0.0)
            qi = qi + jnp.where(keep, ti, 0.0)
            pr, pi = _cmul(pr, pi, pr, pi)
            sh *= 2
        xpr = jnp.where(first, x0r, pltpu.roll(qr, 1, 1))
        xpi = jnp.where(first, x0i, pltpu.roll(qi, 1, 1))
        yt_ref[gi] += (_dot(clr[SSM_GROUP:].astype(BF16), xpr.astype(BF16))
                       - _dot(cli[SSM_GROUP:].astype(BF16), xpi.astype(BF16)))

    d_row = d_ref[0]
    for b in range(n_batch):
        for t in range(S5_CHUNK):
            piece = jnp.concatenate(
                [yt_ref[g, t * SSM_GROUP:(t + 1) * SSM_GROUP, b * n_chunks:(b + 1) * n_chunks]
                 for g in range(gb)], axis=0)
            rows = pl.ds(t, n_chunks, stride=S5_CHUNK)
            y_ref[b, rows, :] = piece.T + d_row * u_ref[b, rows, :]


def _s5_mixer(proj3, u_meta, lam_re, lam_im, log_dt, b_re, b_im, c_re, c_im, d_skip, side_weights):
    n_batch, seq, _ = proj3.shape
    n_chunks = seq // S5_CHUNK
    assert n_chunks == LANES
    g = N_GROUPS
    gb = S5_GROUP_BLOCK
    um = u_meta.reshape(S5_CHUNK, g, SSM_GROUP).transpose(1, 0, 2).reshape(g, 1, S5_ROWS)
    lr = lam_re.reshape(g, 1, SSM_STATE)
    li = lam_im.reshape(g, 1, SSM_STATE)
    lrc = lam_re.reshape(g, SSM_STATE, 1)
    lic = lam_im.reshape(g, SSM_STATE, 1)
    ldt = log_dt.reshape(g, 1, 1)
    bre = jnp.tile(b_re, (1, 1, S5_CHUNK))
    bim = jnp.tile(b_im, (1, 1, S5_CHUNK))
    dsk = d_skip.reshape(g // gb, 1, LANES)

    def spec(shape):
        return pl.BlockSpec((gb,) + shape, lambda i: (i, 0, 0))

    slab = pl.BlockSpec((n_batch, seq, LANES), lambda i: (0, 0, i))
    n_steps = g // gb
    side_specs = [pl.BlockSpec((w.shape[0] // n_steps, w.shape[1]), lambda i: (i, 0))
                  for w in side_weights]
    kern = functools.partial(_s5_kernel, n_batch=n_batch, n_chunks=n_chunks)
    return pl.pallas_call(
        kern,
        grid=(n_steps,),
        in_specs=[slab,
                  spec((1, S5_ROWS)),
                  spec((1, SSM_STATE)), spec((1, SSM_STATE)),
                  spec((SSM_STATE, 1)), spec((SSM_STATE, 1)),
                  spec((1, 1)),
                  spec((SSM_STATE, S5_ROWS)), spec((SSM_STATE, S5_ROWS)),
                  spec((SSM_GROUP, SSM_STATE)), spec((SSM_GROUP, SSM_STATE)),
                  pl.BlockSpec((1, 1, LANES), lambda i: (i, 0, 0))] + side_specs,
        out_specs=[slab] + side_specs,
        out_shape=[jax.ShapeDtypeStruct((n_batch, seq, D_SSM), F32)]
        + [jax.ShapeDtypeStruct(w.shape, BF16) for w in side_weights],
        scratch_shapes=[pltpu.VMEM((gb, S5_ROWS, n_batch * n_chunks), F32),
                        pltpu.VMEM((gb, S5_ROWS, n_batch * n_chunks), F32),
                        pltpu.VMEM((2 * S5_ROWS, S5_ROWS), F32),
                        pltpu.VMEM((S5_ROWS + SSM_GROUP, SSM_STATE), F32),
                        pltpu.VMEM((S5_ROWS + SSM_GROUP, SSM_STATE), F32)],
        compiler_params=_cparams(("parallel",)),
        name="s5_mixer",
    )(proj3, um, lr, li, lrc, lic, ldt, bre, bim, c_re, c_im, dsk, *side_weights)


def _gelu_tanh(x):
    c = math.sqrt(2.0 / math.pi)
    return x * (0.5 * (1.0 + jnp.tanh(c * (x + 0.044715 * (x * x * x)))))


GLU_K_CHUNK = 256


def _glu_kernel(y_ref, wa_ref, wb_ref, o_ref):
    a = None
    b = None
    for k in range(y_ref.shape[1] // GLU_K_CHUNK):
        cols = slice(k * GLU_K_CHUNK, (k + 1) * GLU_K_CHUNK)
        z = _gelu_tanh(y_ref[:, cols]).astype(BF16)
        pa = _dot(z, wa_ref[cols, :])
        pb = _dot(z, wb_ref[cols, :])
        a = pa if a is None else a + pa
        b = pb if b is None else b + pb
    o_ref[...] = (a * _sigmoid(b)).astype(o_ref.dtype)


def _glu_proj(y, w_glu, tm, tn):
    t, k = y.shape
    n = w_glu.shape[1] // 2
    nb = n // tn
    return pl.pallas_call(
        _glu_kernel,
        grid=(t // tm, nb),
        in_specs=[pl.BlockSpec((tm, k), lambda i, j: (i, 0)),
                  pl.BlockSpec((k, tn), lambda i, j: (0, j)),
                  pl.BlockSpec((k, tn), lambda i, j: (0, j + nb))],
        out_specs=pl.BlockSpec((tm, tn), lambda i, j: (i, j)),
        out_shape=jax.ShapeDtypeStruct((t, n), BF16),
        compiler_params=_cparams(("parallel", "parallel")),
        name="s5_glu_proj",
    )(y, w_glu, w_glu)


CONV_HALO = 32
CONV_ROWS = 64
CONV_SLABS = D_CONV // LANES


def _conv_kernel(c1_ref, c2_ref, m1_ref, m2_ref, w_ref, b_ref, lg_ref, lb_ref, wo_ref,
                 w1_ref, w2_ref, o_ref, w1b_ref, w2b_ref, cbuf_ref, conv_ref, act_ref, *, tr):
    r = pl.program_id(1)

    w1b_ref[...] = w1_ref[...].astype(BF16)
    w2b_ref[...] = w2_ref[...].astype(BF16)

    @pl.when(r == 0)
    def _():
        cm = m1_ref[...] * _sigmoid(m2_ref[...])
        for j in range(CONV_SLABS):
            cbuf_ref[j, 0:CONV_HALO - N_META, :] = jnp.zeros((CONV_HALO - N_META, LANES), F32)
            cbuf_ref[j, CONV_HALO - N_META:CONV_HALO, :] = cm[:, j * LANES:(j + 1) * LANES]

    @pl.when(r > 0)
    def _():
        cbuf_ref[:, 0:CONV_HALO, :] = cbuf_ref[:, tr:tr + CONV_HALO, :]

    c = c1_ref[...] * _sigmoid(c2_ref[...])
    for j in range(CONV_SLABS):
        cbuf_ref[j, CONV_HALO:, :] = c[:, j * LANES:(j + 1) * LANES]

    off = CONV_HALO - (CONV_WIDTH - 1)
    half = CONV_ROWS // 2

    def chunk(rc, carry):
        base = pl.multiple_of(rc * CONV_ROWS, CONV_ROWS)
        for j in range(CONV_SLABS):
            lanes = slice(j * LANES, (j + 1) * LANES)
            bias = jnp.broadcast_to(b_ref[:, lanes], (half, LANES))
            acc_e = bias
            acc_o = bias
            for k in range(CONV_WIDTH):
                wk = w_ref[k:k + 1, lanes]
                acc_e = acc_e + wk * cbuf_ref[j, pl.ds(base + off + k, half, stride=2), :]
                acc_o = acc_o + wk * cbuf_ref[j, pl.ds(base + off + k + 1, half, stride=2), :]
            conv_ref[j, pl.ds(base, half, stride=2), :] = acc_e
            conv_ref[j, pl.ds(base + 1, half, stride=2), :] = acc_o
        return carry

    lax.fori_loop(0, tr // CONV_ROWS, chunk, 0)

    s1 = jnp.zeros((tr, 1), F32)
    for j in range(CONV_SLABS):
        s1 = s1 + jnp.sum(conv_ref[j], axis=-1, keepdims=True)
    mu = s1 * (1.0 / D_CONV)
    s2 = jnp.zeros((tr, 1), F32)
    for j in range(CONV_SLABS):
        vc = conv_ref[j] - mu
        s2 = s2 + jnp.sum(vc * vc, axis=-1, keepdims=True)
    inv = lax.rsqrt(s2 * (1.0 / D_CONV) + EPS)
    for j in range(CONV_SLABS):
        lanes = slice(j * LANES, (j + 1) * LANES)
        yln = (conv_ref[j] - mu) * inv * lg_ref[:, lanes] + lb_ref[:, lanes]
        act_ref[:, lanes] = (yln * _sigmoid(yln)).astype(BF16)
    o_ref[...] = _dot(act_ref[...], wo_ref[...]).astype(o_ref.dtype)


def _conv_branch(proj, proj_meta, conv_w, conv_b, ln_g, ln_b, w_out, w_ff1, w_ff2, n_batch, seq, tr):
    nr = seq // tr
    n_steps = n_batch * nr
    kern = functools.partial(_conv_kernel, tr=tr)
    vec = lambda: pl.BlockSpec((1, D_CONV), lambda b, r: (0, 0))
    side = lambda w: pl.BlockSpec((w.shape[0] // n_steps, w.shape[1]), lambda b, r: (b * nr + r, 0))
    return pl.pallas_call(
        kern,
        grid=(n_batch, nr),
        in_specs=[pl.BlockSpec((tr, D_CONV), lambda b, r: (b * nr + r, 1)),
                  pl.BlockSpec((tr, D_CONV), lambda b, r: (b * nr + r, 2)),
                  pl.BlockSpec((N_META, D_CONV), lambda b, r: (0, 1)),
                  pl.BlockSpec((N_META, D_CONV), lambda b, r: (0, 2)),
                  pl.BlockSpec((CONV_WIDTH, D_CONV), lambda b, r: (0, 0)),
                  vec(), vec(), vec(),
                  pl.BlockSpec((D_CONV, D_MODEL), lambda b, r: (0, 0)),
                  side(w_ff1), side(w_ff2)],
        out_specs=[pl.BlockSpec((tr, D_MODEL), lambda b, r: (b * nr + r, 0)),
                   side(w_ff1), side(w_ff2)],
        out_shape=[jax.ShapeDtypeStruct((n_batch * seq, D_MODEL), BF16),
                   jax.ShapeDtypeStruct(w_ff1.shape, BF16),
                   jax.ShapeDtypeStruct(w_ff2.shape, BF16)],
        scratch_shapes=[pltpu.VMEM((CONV_SLABS, CONV_HALO + tr, LANES), F32),
                        pltpu.VMEM((CONV_SLABS, tr, LANES), F32),
                        pltpu.VMEM((tr, D_CONV), BF16)],
        compiler_params=_cparams(("parallel", "arbitrary")),
        name="conv_branch",
    )(proj, proj, proj_meta, proj_meta, conv_w, conv_b, ln_g, ln_b, w_out, w_ff1, w_ff2)


MERGE_K_CHUNK = 512


def _merge_kernel(oa_ref, ob_ref, ga_ref, gb_ref, x_ref, w_ref, o_ref):
    acc = x_ref[...]
    for k in range(D_MODEL // MERGE_K_CHUNK):
        cols = slice(k * MERGE_K_CHUNK, (k + 1) * MERGE_K_CHUNK)
        m = (ga_ref[:, cols].astype(F32) * oa_ref[:, cols].astype(F32)
             + gb_ref[:, cols].astype(F32) * ob_ref[:, cols].astype(F32))
        acc = acc + _dot(m.astype(BF16), w_ref[cols, :])
    o_ref[...] = acc


def _merge_proj(out_a, out_b, gates, x, w_out, tm):
    t, d = x.shape
    row = lambda blk: pl.BlockSpec((tm, d), lambda i: (i, blk))
    return pl.pallas_call(
        _merge_kernel,
        grid=(t // tm,),
        in_specs=[row(0), row(0), row(0), row(1), row(0),
                  pl.BlockSpec((d, d), lambda i: (0, 0))],
        out_specs=row(0),
        out_shape=jax.ShapeDtypeStruct((t, d), F32),
        compiler_params=_cparams(("parallel",)),
        name="merge_out_proj",
    )(out_a, out_b, gates, gates, x, w_out)


def _ffn_kernel(h_ref, g_ref, w1_ref, w2_ref, gf_ref, o_ref, xn_ref, acc_ref):
    f = pl.program_id(1)

    @pl.when(f == 0)
    def _():
        xn_ref[...] = _rms(h_ref[...], g_ref[...]).astype(BF16)
        acc_ref[...] = jnp.zeros_like(acc_ref)

    t = jnp.maximum(_dot(xn_ref[...], w1_ref[...]), 0.0)
    acc_ref[...] += _dot((t * t).astype(BF16), w2_ref[...])

    @pl.when(f == pl.num_programs(1) - 1)
    def _():
        o_ref[...] = _rms(h_ref[...] + acc_ref[...], gf_ref[...])


def _ffn(h, g, w1, w2, gf, tm, tf):
    t, d = h.shape
    ff = w1.shape[1]
    return pl.pallas_call(
        _ffn_kernel,
        grid=(t // tm, ff // tf),
        in_specs=[pl.BlockSpec((tm, d), lambda i, f: (i, 0)),
                  pl.BlockSpec((1, d), lambda i, f: (0, 0)),
                  pl.BlockSpec((d, tf), lambda i, f: (0, f)),
                  pl.BlockSpec((tf, d), lambda i, f: (f, 0)),
                  pl.BlockSpec((1, d), lambda i, f: (0, 0))],
        out_specs=pl.BlockSpec((tm, d), lambda i, f: (i, 0)),
        out_shape=jax.ShapeDtypeStruct((t, d), F32),
        scratch_shapes=[pltpu.VMEM((tm, d), BF16), pltpu.VMEM((tm, d), F32)],
        compiler_params=_cparams(("parallel", "arbitrary")),
        name="ffn_final_norm",
    )(h, g, w1, w2, gf)


def kernel(x, meta, norm_mix_g, w_in, lam_re, lam_im, log_dt, b_re, b_im, c_re, c_im, d_skip,
           w_glu, conv_w, conv_b, conv_ln_g, conv_ln_b, w_conv_out, w_out, norm_ffn_g,
           w_ff1, w_ff2, norm_f_g):
    bsz, seq, d = x.shape
    assert w_in.shape[0] == 1 and d == D_MODEL and meta.shape[0] == N_META
    xr = x.reshape(bsz * seq, d)
    g_mix = norm_mix_g[0].reshape(1, d)

    proj_meta, w_in_b = _meta_proj(meta, g_mix, w_in[0], 1024)
    n_plain = D_SSM + 2 * D_CONV
    proj, gates = _rms_matmul(xr, g_mix, w_in_b, 1024, 1024, n_plain)

    y, w_glu_b, w_out_b, w_co_b = _s5_mixer(
        proj.reshape(bsz, seq, -1), proj_meta[:, :D_SSM], lam_re[0], lam_im[0], log_dt[0],
        b_re[0], b_im[0], c_re[0], c_im[0], d_skip[0], (w_glu[0], w_out[0], w_conv_out[0]))
    y = y.reshape(bsz * seq, D_SSM)

    out_a = _glu_proj(y, w_glu_b, 1024, 1024)

    out_b, w_ff1_b, w_ff2_b = _conv_branch(
        proj, proj_meta, conv_w[0], conv_b[0].reshape(1, -1), conv_ln_g[0].reshape(1, -1),
        conv_ln_b[0].reshape(1, -1), w_co_b, w_ff1[0], w_ff2[0], bsz, seq, 512)

    h1 = _merge_proj(out_a, out_b, gates, xr, w_out_b, 512)

    out = _ffn(h1, norm_ffn_g[0].reshape(1, d), w_ff1_b, w_ff2_b, norm_f_g.reshape(1, d), 512, 1024)
    return out.reshape(bsz, seq, d)
```

```python
import functools
import math

import jax
import jax.numpy as jnp
from jax import lax
from jax.experimental import pallas as pl
from jax.experimental.pallas import tpu as pltpu

F32 = jnp.float32
BF16 = jnp.bfloat16

D_MODEL = 2048
N_META = 16
D_SSM = 1024
SSM_GROUP = 16
N_GROUPS = 64
SSM_STATE = 64
D_CONV = 1024
CONV_WIDTH = 31
D_FF = 8192
EPS = 1e-6
LANES = 128

S5_CHUNK = 16
S5_ROWS = S5_CHUNK * SSM_GROUP
S5_GROUP_BLOCK = LANES // SSM_GROUP

VMEM_LIMIT = 56 * 1024 * 1024


def _cparams(sem):
    return pltpu.CompilerParams(dimension_semantics=sem, vmem_limit_bytes=VMEM_LIMIT)


def _dot(a, b, precision=None):
    return jnp.dot(a, b, preferred_element_type=F32, precision=precision)


def _sigmoid(x):
    return 0.5 * jnp.tanh(0.5 * x) + 0.5


def _rms(x, g):
    ms = jnp.mean(x * x, axis=-1, keepdims=True)
    return x * lax.rsqrt(ms + EPS) * g


IN_PROJ_N_CHUNK = 256


def _rms_matmul_kernel(x_ref, g_ref, w_ref, o_ref, s_ref, xn_ref, *, n_plain):
    j = pl.program_id(1)

    @pl.when(j == 0)
    def _():
        xn_ref[...] = _rms(x_ref[...], g_ref[...]).astype(BF16)

    @pl.when(j < n_plain)
    def _():
        o_ref[...] = _dot(xn_ref[...], w_ref[...])

    @pl.when(j >= n_plain)
    def _():
        for c in range(w_ref.shape[1] // IN_PROJ_N_CHUNK):
            cols = slice(c * IN_PROJ_N_CHUNK, (c + 1) * IN_PROJ_N_CHUNK)
            s_ref[:, cols] = _sigmoid(_dot(xn_ref[...], w_ref[:, cols])).astype(BF16)


def _rms_matmul(x, g, w, tm, tn, n_plain_cols):
    t, d = x.shape
    n = w.shape[1]
    n_plain = n_plain_cols // tn
    kern = functools.partial(_rms_matmul_kernel, n_plain=n_plain)
    return pl.pallas_call(
        kern,
        grid=(t // tm, n // tn),
        in_specs=[pl.BlockSpec((tm, d), lambda i, j: (i, 0)),
                  pl.BlockSpec((1, d), lambda i, j: (0, 0)),
                  pl.BlockSpec((d, tn), lambda i, j: (0, j))],
        out_specs=[pl.BlockSpec((tm, tn), lambda i, j: (i, jnp.minimum(j, n_plain - 1))),
                   pl.BlockSpec((tm, tn), lambda i, j: (i, jnp.maximum(j - n_plain, 0)))],
        out_shape=[jax.ShapeDtypeStruct((t, n_plain_cols), F32),
                   jax.ShapeDtypeStruct((t, n - n_plain_cols), BF16)],
        scratch_shapes=[pltpu.VMEM((tm, d), BF16)],
        compiler_params=_cparams(("parallel", "arbitrary")),
        name="rms_in_proj",
    )(x, g, w)


def _meta_proj_kernel(x_ref, g_ref, w_ref, o_ref, wb_ref, xn_ref):
    @pl.when(pl.program_id(0) == 0)
    def _():
        xn_ref[...] = _rms(x_ref[...], g_ref[...]).astype(BF16)

    wb = w_ref[...].astype(BF16)
    wb_ref[...] = wb
    o_ref[...] = _dot(xn_ref[...], wb)


def _meta_proj(meta, g, w, tn):
    t, d = meta.shape
    n = w.shape[1]
    return pl.pallas_call(
        _meta_proj_kernel,
        grid=(n // tn,),
        in_specs=[pl.BlockSpec((t, d), lambda j: (0, 0)),
                  pl.BlockSpec((1, d), lambda j: (0, 0)),
                  pl.BlockSpec((d, tn), lambda j: (0, j))],
        out_specs=[pl.BlockSpec((t, tn), lambda j: (0, j)),
                   pl.BlockSpec((d, tn), lambda j: (0, j))],
        out_shape=[jax.ShapeDtypeStruct((t, n), F32), jax.ShapeDtypeStruct((d, n), BF16)],
        scratch_shapes=[pltpu.VMEM((t, d), BF16)],
        compiler_params=_cparams(("arbitrary",)),
        name="meta_in_proj",
    )(meta, g, w)


def _cmul(ar, ai, br, bi):
    return ar * br - ai * bi, ar * bi + ai * br


def _cpow_table(br, bi, k, n_bits):
    bit = (k & 1) > 0
    pr = jnp.where(bit, br, 1.0)
    pi = jnp.where(bit, bi, 0.0)
    for j in range(1, n_bits):
        br, bi = _cmul(br, bi, br, bi)
        tr, ti = _cmul(pr, pi, br, bi)
        bit = ((k >> j) & 1) > 0
        pr = jnp.where(bit, tr, pr)
        pi = jnp.where(bit, ti, pi)
    return pr, pi


def _dot3(a, b):
    ah = a.astype(BF16)
    al = (a - ah.astype(F32)).astype(BF16)
    bh = b.astype(BF16)
    bl = (b - bh.astype(F32)).astype(BF16)
    return _dot(ah, bh) + (_dot(ah, bl) + _dot(al, bh))


def _s5_kernel(u_ref, um_ref, lr_ref, li_ref, lrc_ref, lic_ref, ldt_ref, bre_ref, bim_ref,
               cre_ref, cim_ref, d_ref, wa_ref, wb_ref, wc_ref,
               y_ref, wab_ref, wbb_ref, wcb_ref, ut_ref, yt_ref, z_ref, clr_ref, cli_ref,
               *, n_batch, n_chunks):
    gb = S5_GROUP_BLOCK
    n_lanes = n_batch * n_chunks

    wab_ref[...] = wa_ref[...].astype(BF16)
    wbb_ref[...] = wb_ref[...].astype(BF16)
    wcb_ref[...] = wc_ref[...].astype(BF16)

    for b in range(n_batch):
        for s in range(S5_CHUNK):
            piece = u_ref[b, pl.ds(s, n_chunks, stride=S5_CHUNK), :]
            pt = piece.T
            for g in range(gb):
                ut_ref[g, s * SSM_GROUP:(s + 1) * SSM_GROUP, b * n_chunks:(b + 1) * n_chunks] = (
                    pt[g * SSM_GROUP:(g + 1) * SSM_GROUP, :])

    lane_blk = lax.broadcasted_iota(jnp.int32, (1, S5_ROWS), 1) // SSM_GROUP
    kq = S5_CHUNK - 1 - lane_blk
    lane_c = lax.broadcasted_iota(jnp.int32, (1, n_lanes), 1) % n_chunks
    first = lane_c == 0
    z_ref[0:S5_ROWS, :] = jnp.zeros((S5_ROWS, S5_ROWS), F32)

    for gi in range(gb):
        dt = jnp.exp(ldt_ref[gi])
        lrc = lrc_ref[gi]
        lic = lic_ref[gi]
        ar = lrc * dt
        ai = lic * dt
        e = jnp.exp(ar)
        lbr = e * jnp.cos(ai)
        lbi = e * jnp.sin(ai)
        den = lrc * lrc + lic * lic
        nr = lbr - 1.0
        fr = (nr * lrc + lbi * lic) / den
        fi = (lbi * lrc - nr * lic) / den
        bbr, bbi = _cmul(fr, fi, bre_ref[gi], bim_ref[gi])
        qpr, qpi = _cpow_table(lbr, lbi, kq, 4)
        qtr, qti = _cmul(qpr, qpi, bbr, bbi)

        er = jnp.exp(lr_ref[gi] * dt)
        air = li_ref[gi] * dt
        lrr = er * jnp.cos(air)
        lri = er * jnp.sin(air)
        cr = cre_ref[gi]
        ci = cim_ref[gi]
        for tau in range(S5_CHUNK + 1):
            clr_ref[tau * SSM_GROUP:(tau + 1) * SSM_GROUP, :] = cr
            cli_ref[tau * SSM_GROUP:(tau + 1) * SSM_GROUP, :] = ci
            cr, ci = _cmul(cr, ci, lrr, lri)
        clr = clr_ref[...]
        cli = cli_ref[...]

        kt = _dot3(clr[:S5_ROWS], bbr) - _dot3(cli[:S5_ROWS], bbi)
        z_ref[S5_ROWS:, :] = kt
        mt = jnp.zeros((S5_ROWS, S5_ROWS), F32)
        for s in range(S5_CHUNK):
            lo = S5_ROWS - SSM_GROUP * s
            mt = jnp.where(lane_blk == s, z_ref[lo:lo + S5_ROWS, :], mt)

        ut = ut_ref[gi]
        utb = ut.astype(BF16)
        yt_ref[gi] = _dot(mt.astype(BF16), utb)

        um = um_ref[gi]
        x0r = jnp.sum(qtr * um, axis=1, keepdims=True)
        x0i = jnp.sum(qti * um, axis=1, keepdims=True)
        pr, pi = lbr, lbi
        for _ in range(4):
            pr, pi = _cmul(pr, pi, pr, pi)
        axr, axi = _cmul(pr, pi, x0r, x0i)
        qr = _dot(qtr.astype(BF16), utb)
        qi = _dot(qti.astype(BF16), utb)
        qr = jnp.where(first, qr + axr, qr)
        qi = jnp.where(first, qi + axi, qi)
        sh = 1
        while sh < n_chunks:
            rr = pltpu.roll(qr, sh, 1)
            ri = pltpu.roll(qi, sh, 1)
            tr, ti = _cmul(pr, pi, rr, ri)
            keep = lane_c >= sh
            qr = qr + jnp.where(keep, tr, 0.0)
            qi = qi + jnp.where(keep, ti, 0.0)
            pr, pi = _cmul(pr, pi, pr, pi)
            sh *= 2
        xpr = jnp.where(first, x0r, pltpu.roll(qr, 1, 1))
        xpi = jnp.where(first, x0i, pltpu.roll(qi, 1, 1))
        yt_ref[gi] += (_dot(clr[SSM_GROUP:].astype(BF16), xpr.astype(BF16))
                       - _dot(cli[SSM_GROUP:].astype(BF16), xpi.astype(BF16)))

    d_row = d_ref[0]
    for b in range(n_batch):
        for t in range(S5_CHUNK):
            piece = jnp.concatenate(
                [yt_ref[g, t * SSM_GROUP:(t + 1) * SSM_GROUP, b * n_chunks:(b + 1) * n_chunks]
                 for g in range(gb)], axis=0)
            rows = pl.ds(t, n_chunks, stride=S5_CHUNK)
            y_ref[b, rows, :] = piece.T + d_row * u_ref[b, rows, :]


def _s5_mixer(proj3, u_meta, lam_re, lam_im, log_dt, b_re, b_im, c_re, c_im, d_skip, side_weights):
    n_batch, seq, _ = proj3.shape
    n_chunks = seq // S5_CHUNK
    assert n_chunks == LANES
    g = N_GROUPS
    gb = S5_GROUP_BLOCK
    um = u_meta.reshape(S5_CHUNK, g, SSM_GROUP).transpose(1, 0, 2).reshape(g, 1, S5_ROWS)
    lr = lam_re.reshape(g, 1, SSM_STATE)
    li = lam_im.reshape(g, 1, SSM_STATE)
    lrc = lam_re.reshape(g, SSM_STATE, 1)
    lic = lam_im.reshape(g, SSM_STATE, 1)
    ldt = log_dt.reshape(g, 1, 1)
    bre = jnp.tile(b_re, (1, 1, S5_CHUNK))
    bim = jnp.tile(b_im, (1, 1, S5_CHUNK))
    dsk = d_skip.reshape(g // gb, 1, LANES)

    def spec(shape):
        return pl.BlockSpec((gb,) + shape, lambda i: (i, 0, 0))

    slab = pl.BlockSpec((n_batch, seq, LANES), lambda i: (0, 0, i))
    n_steps = g // gb
    side_specs = [pl.BlockSpec((w.shape[0] // n_steps, w.shape[1]), lambda i: (i, 0))
                  for w in side_weights]
    kern = functools.partial(_s5_kernel, n_batch=n_batch, n_chunks=n_chunks)
    return pl.pallas_call(
        kern,
        grid=(n_steps,),
        in_specs=[slab,
                  spec((1, S5_ROWS)),
                  spec((1, SSM_STATE)), spec((1, SSM_STATE)),
                  spec((SSM_STATE, 1)), spec((SSM_STATE, 1)),
                  spec((1, 1)),
                  spec((SSM_STATE, S5_ROWS)), spec((SSM_STATE, S5_ROWS)),
                  spec((SSM_GROUP, SSM_STATE)), spec((SSM_GROUP, SSM_STATE)),
                  pl.BlockSpec((1, 1, LANES), lambda i: (i, 0, 0))] + side_specs,
        out_specs=[slab] + side_specs,
        out_shape=[jax.ShapeDtypeStruct((n_batch, seq, D_SSM), F32)]
        + [jax.ShapeDtypeStruct(w.shape, BF16) for w in side_weights],
        scratch_shapes=[pltpu.VMEM((gb, S5_ROWS, n_batch * n_chunks), F32),
                        pltpu.VMEM((gb, S5_ROWS, n_batch * n_chunks), F32),
                        pltpu.VMEM((2 * S5_ROWS, S5_ROWS), F32),
                        pltpu.VMEM((S5_ROWS + SSM_GROUP, SSM_STATE), F32),
                        pltpu.VMEM((S5_ROWS + SSM_GROUP, SSM_STATE), F32)],
        compiler_params=_cparams(("parallel",)),
        name="s5_mixer",
    )(proj3, um, lr, li, lrc, lic, ldt, bre, bim, c_re, c_im, dsk, *side_weights)


def _gelu_tanh(x):
    c = math.sqrt(2.0 / math.pi)
    return x * (0.5 * (1.0 + jnp.tanh(c * (x + 0.044715 * (x * x * x)))))


GLU_K_CHUNK = 256


def _glu_kernel(y_ref, wa_ref, wb_ref, o_ref):
    a = None
    b = None
    for k in range(y_ref.shape[1] // GLU_K_CHUNK):
        cols = slice(k * GLU_K_CHUNK, (k + 1) * GLU_K_CHUNK)
        z = _gelu_tanh(y_ref[:, cols]).astype(BF16)
        pa = _dot(z, wa_ref[cols, :])
        pb = _dot(z, wb_ref[cols, :])
        a = pa if a is None else a + pa
        b = pb if b is None else b + pb
    o_ref[...] = (a * _sigmoid(b)).astype(o_ref.dtype)


def _glu_proj(y, w_glu, tm, tn):
    t, k = y.shape
    n = w_glu.shape[1] // 2
    nb = n // tn
    return pl.pallas_call(
        _glu_kernel,
        grid=(t // tm, nb),
        in_specs=[pl.BlockSpec((tm, k), lambda i, j: (i, 0)),
                  pl.BlockSpec((k, tn), lambda i, j: (0, j)),
                  pl.BlockSpec((k, tn), lambda i, j: (0, j + nb))],
        out_specs=pl.BlockSpec((tm, tn), lambda i, j: (i, j)),
        out_shape=jax.ShapeDtypeStruct((t, n), BF16),
        compiler_params=_cparams(("parallel", "parallel")),
        name="s5_glu_proj",
    )(y, w_glu, w_glu)


CONV_HALO = 32
CONV_ROWS = 64
CONV_SLABS = D_CONV // LANES


def _conv_kernel(c1_ref, c2_ref, m1_ref, m2_ref, w_ref, b_ref, lg_ref, lb_ref, wo_ref,
                 w1_ref, w2_ref, o_ref, w1b_ref, w2b_ref, cbuf_ref, conv_ref, act_ref, *, tr):
    r = pl.program_id(1)

    w1b_ref[...] = w1_ref[...].astype(BF16)
    w2b_ref[...] = w2_ref[...].astype(BF16)

    @pl.when(r == 0)
    def _():
        cm = m1_ref[...] * _sigmoid(m2_ref[...])
        for j in range(CONV_SLABS):
            cbuf_ref[j, 0:CONV_HALO - N_META, :] = jnp.zeros((CONV_HALO - N_META, LANES), F32)
            cbuf_ref[j, CONV_HALO - N_META:CONV_HALO, :] = cm[:, j * LANES:(j + 1) * LANES]

    @pl.when(r > 0)
    def _():
        cbuf_ref[:, 0:CONV_HALO, :] = cbuf_ref[:, tr:tr + CONV_HALO, :]

    c = c1_ref[...] * _sigmoid(c2_ref[...])
    for j in range(CONV_SLABS):
        cbuf_ref[j, CONV_HALO:, :] = c[:, j * LANES:(j + 1) * LANES]

    off = CONV_HALO - (CONV_WIDTH - 1)
    half = CONV_ROWS // 2

    def chunk(rc, carry):
        base = pl.multiple_of(rc * CONV_ROWS, CONV_ROWS)
        for j in range(CONV_SLABS):
            lanes = slice(j * LANES, (j + 1) * LANES)
            bias = jnp.broadcast_to(b_ref[:, lanes], (half, LANES))
            acc_e = bias
            acc_o = bias
            for k in range(CONV_WIDTH):
                wk = w_ref[k:k + 1, lanes]
                acc_e = acc_e + wk * cbuf_ref[j, pl.ds(base + off + k, half, stride=2), :]
                acc_o = acc_o + wk * cbuf_ref[j, pl.ds(base + off + k + 1, half, stride=2), :]
            conv_ref[j, pl.ds(base, half, stride=2), :] = acc_e
            conv_ref[j, pl.ds(base + 1, half, stride=2), :] = acc_o
        return carry

    lax.fori_loop(0, tr // CONV_ROWS, chunk, 0)

    s1 = jnp.zeros((tr, 1), F32)
    for j in range(CONV_SLABS):
        s1 = s1 + jnp.sum(conv_ref[j], axis=-1, keepdims=True)
    mu = s1 * (1.0 / D_CONV)
    s2 = jnp.zeros((tr, 1), F32)
    for j in range(CONV_SLABS):
        vc = conv_ref[j] - mu
        s2 = s2 + jnp.sum(vc * vc, axis=-1, keepdims=True)
    inv = lax.rsqrt(s2 * (1.0 / D_CONV) + EPS)
    for j in range(CONV_SLABS):
        lanes = slice(j * LANES, (j + 1) * LANES)
        yln = (conv_ref[j] - mu) * inv * lg_ref[:, lanes] + lb_ref[:, lanes]
        act_ref[:, lanes] = (yln * _sigmoid(yln)).astype(BF16)
    o_ref[...] = _dot(act_ref[...], wo_ref[...]).astype(o_ref.dtype)


def _conv_branch(proj, proj_meta, conv_w, conv_b, ln_g, ln_b, w_out, w_ff1, w_ff2, n_batch, seq, tr):
    nr = seq // tr
    n_steps = n_batch * nr
    kern = functools.partial(_conv_kernel, tr=tr)
    vec = lambda: pl.BlockSpec((1, D_CONV), lambda b, r: (0, 0))
    side = lambda w: pl.BlockSpec((w.shape[0] // n_steps, w.shape[1]), lambda b, r: (b * nr + r, 0))
    return pl.pallas_call(
        kern,
        grid=(n_batch, nr),
        in_specs=[pl.BlockSpec((tr, D_CONV), lambda b, r: (b * nr + r, 1)),
                  pl.BlockSpec((tr, D_CONV), lambda b, r: (b * nr + r, 2)),
                  pl.BlockSpec((N_META, D_CONV), lambda b, r: (0, 1)),
                  pl.BlockSpec((N_META, D_CONV), lambda b, r: (0, 2)),
                  pl.BlockSpec((CONV_WIDTH, D_CONV), lambda b, r: (0, 0)),
                  vec(), vec(), vec(),
                  pl.BlockSpec((D_CONV, D_MODEL), lambda b, r: (0, 0)),
                  side(w_ff1), side(w_ff2)],
        out_specs=[pl.BlockSpec((tr, D_MODEL), lambda b, r: (b * nr + r, 0)),
                   side(w_ff1), side(w_ff2)],
        out_shape=[jax.ShapeDtypeStruct((n_batch * seq, D_MODEL), BF16),
                   jax.ShapeDtypeStruct(w_ff1.shape, BF16),
                   jax.ShapeDtypeStruct(w_ff2.shape, BF16)],
        scratch_shapes=[pltpu.VMEM((CONV_SLABS, CONV_HALO + tr, LANES), F32),
                        pltpu.VMEM((CONV_SLABS, tr, LANES), F32),
                        pltpu.VMEM((tr, D_CONV), BF16)],
        compiler_params=_cparams(("parallel", "arbitrary")),
        name="conv_branch",
    )(proj, proj, proj_meta, proj_meta, conv_w, conv_b, ln_g, ln_b, w_out, w_ff1, w_ff2)


MERGE_K_CHUNK = 512


def _merge_kernel(oa_ref, ob_ref, ga_ref, gb_ref, x_ref, w_ref, o_ref):
    acc = x_ref[...]
    for k in range(D_MODEL // MERGE_K_CHUNK):
        cols = slice(k * MERGE_K_CHUNK, (k + 1) * MERGE_K_CHUNK)
        m = (ga_ref[:, cols].astype(F32) * oa_ref[:, cols].astype(F32)
             + gb_ref[:, cols].astype(F32) * ob_ref[:, cols].astype(F32))
        acc = acc + _dot(m.astype(BF16), w_ref[cols, :])
    o_ref[...] = acc


def _merge_proj(out_a, out_b, gates, x, w_out, tm):
    t, d = x.shape
    row = lambda blk: pl.BlockSpec((tm, d), lambda i: (i, blk))
    return pl.pallas_call(
        _merge_kernel,
        grid=(t // tm,),
        in_specs=[row(0), row(0), row(0), row(1), row(0),
                  pl.BlockSpec((d, d), lambda i: (0, 0))],
        out_specs=row(0),
        out_shape=jax.ShapeDtypeStruct((t, d), F32),
        compiler_params=_cparams(("parallel",)),
        name="merge_out_proj",
    )(out_a, out_b, gates, gates, x, w_out)


def _ffn_kernel(h_ref, g_ref, w1_ref, w2_ref, gf_ref, o_ref, xn_ref):
    f = pl.program_id(1)

    @pl.when(f == 0)
    def _():
        h = h_ref[...]
        xn_ref[...] = _rms(h, g_ref[...]).astype(BF16)
        o_ref[...] = h

    t = jnp.maximum(_dot(xn_ref[...], w1_ref[...]), 0.0)
    o_ref[...] += _dot((t * t).astype(BF16), w2_ref[...])

    @pl.when(f == pl.num_programs(1) - 1)
    def _():
        o_ref[...] = _rms(o_ref[...], gf_ref[...])


def _ffn(h, g, w1, w2, gf, tm, tf):
    t, d = h.shape
    ff = w1.shape[1]
    return pl.pallas_call(
        _ffn_kernel,
        grid=(t // tm, ff // tf),
        in_specs=[pl.BlockSpec((tm, d), lambda i, f: (i, 0)),
                  pl.BlockSpec((1, d), lambda i, f: (0, 0)),
                  pl.BlockSpec((d, tf), lambda i, f: (0, f)),
                  pl.BlockSpec((tf, d), lambda i, f: (f, 0)),
                  pl.BlockSpec((1, d), lambda i, f: (0, 0))],
        out_specs=pl.BlockSpec((tm, d), lambda i, f: (i, 0)),
        out_shape=jax.ShapeDtypeStruct((t, d), F32),
        scratch_shapes=[pltpu.VMEM((tm, d), BF16)],
        compiler_params=_cparams(("parallel", "arbitrary")),
        name="ffn_final_norm",
    )(h, g, w1, w2, gf)


def kernel(x, meta, norm_mix_g, w_in, lam_re, lam_im, log_dt, b_re, b_im, c_re, c_im, d_skip,
           w_glu, conv_w, conv_b, conv_ln_g, conv_ln_b, w_conv_out, w_out, norm_ffn_g,
           w_ff1, w_ff2, norm_f_g):
    bsz, seq, d = x.shape
    assert w_in.shape[0] == 1 and d == D_MODEL and meta.shape[0] == N_META
    xr = x.reshape(bsz * seq, d)
    g_mix = norm_mix_g[0].reshape(1, d)

    proj_meta, w_in_b = _meta_proj(meta, g_mix, w_in[0], 1024)
    n_plain = D_SSM + 2 * D_CONV
    proj, gates = _rms_matmul(xr, g_mix, w_in_b, 1024, 1024, n_plain)

    y, w_glu_b, w_out_b, w_co_b = _s5_mixer(
        proj.reshape(bsz, seq, -1), proj_meta[:, :D_SSM], lam_re[0], lam_im[0], log_dt[0],
        b_re[0], b_im[0], c_re[0], c_im[0], d_skip[0], (w_glu[0], w_out[0], w_conv_out[0]))
    y = y.reshape(bsz * seq, D_SSM)

    out_a = _glu_proj(y, w_glu_b, 1024, 1024)

    out_b, w_ff1_b, w_ff2_b = _conv_branch(
        proj, proj_meta, conv_w[0], conv_b[0].reshape(1, -1), conv_ln_g[0].reshape(1, -1),
        conv_ln_b[0].reshape(1, -1), w_co_b, w_ff1[0], w_ff2[0], bsz, seq, 512)

    h1 = _merge_proj(out_a, out_b, gates, xr, w_out_b, 512)

    out = _ffn(h1, norm_ffn_g[0].reshape(1, d), w_ff1_b, w_ff2_b, norm_f_g.reshape(1, d), 1024, 512)
    return out.reshape(bsz, seq, d)
```
